```python
import math
import jax
import jax.numpy as jnp
from jax import lax
import numpy as np

D_MODEL = 2048
BATCH = 4
SEQ = 2048
DEPTH = 4
DEC_BATCH = 8
DEC_SEQ = 8
PAST_LEN = 16384
PAGE_SIZE = 128

D_MIX = D_MODEL
GROUP_W = D_MIX // 4
DN_HEADS = 4
DN_DK = GROUP_W // DN_HEADS
DN_DV = GROUP_W // DN_HEADS
DN_CONV = 4
DN_CHUNK = 64
SSM_CH = 16
SSM_GROUPS = GROUP_W // SSM_CH
SSM_STATE = 64
SWA_HEADS = 8
SWA_HD = GROUP_W // SWA_HEADS
DILATED = ((128, 1), (512, 4), (2048, 16))
WIN_MAX = 2048
SWA_QBLOCK = 128
REL_BUCKETS = 32
REL_MAX_DIST = 2048
LRU_BLOCKS = 8
LRU_BW = GROUP_W // LRU_BLOCKS
LRU_CONV = 4
LRU_C = 8.0
MEM_LEN = 256
MEM_HEADS = 4
MEM_HD = 128
D_FF = 5632
FFN_CONV = 3
EPS = 1e-6
IN_SIZES = (3 * GROUP_W, GROUP_W, DN_HEADS, DN_HEADS, GROUP_W, 3 * GROUP_W, GROUP_W, GROUP_W)
IN_W = sum(IN_SIZES)

kernel_name = 'hybrid_parallel_heads_decoder_step'


def in_split_points():
    pts, acc = [], 0
    for s in IN_SIZES[:-1]:
        acc += s
        pts.append(acc)
    return pts


def rmsnorm(x, g):
    x32 = x.astype(jnp.float32)
    y = x32 * lax.rsqrt(jnp.mean(x32 * x32, axis=-1, keepdims=True) + EPS)
    return (y * g.astype(jnp.float32)).astype(x.dtype)


def l2norm(x):
    return x * lax.rsqrt(jnp.sum(x * x, axis=-1, keepdims=True) + EPS)


def causal_dwconv(x, buf, w):
    kw, t = w.shape[0], x.shape[1]
    xx = jnp.concatenate([buf.astype(x.dtype), x], axis=1)
    y = xx[:, 0:t] * w[0]
    for j in range(1, kw):
        y = y + xx[:, j:j + t] * w[j]
    return y, xx[:, t:]


def linear_combine(e1, e2):
    a1, b1 = e1
    a2, b2 = e2
    return a2 * a1, a2 * b1 + b2


def complex_combine(e1, e2):
    a1r, a1i, b1r, b1i = e1
    a2r, a2i, b2r, b2i = e2
    return (a2r * a1r - a2i * a1i, a2r * a1i + a2i * a1r,
            a2r * b1r - a2i * b1i + b2r, a2r * b1i + a2i * b1r + b2i)


def chunk_gated_delta(q, k, v, g, beta, s0):
    bsz, t, h, dk = q.shape
    dv = v.shape[-1]
    c = DN_CHUNK
    n = -(-t // c)
    pad = n * c - t

    def to_chunks(z):
        z = jnp.pad(z, [(0, 0), (0, pad)] + [(0, 0)] * (z.ndim - 2))
        z = z.reshape((bsz, n, c) + z.shape[2:])
        return jnp.transpose(z, (1, 0, 3, 2) + tuple(range(4, z.ndim)))

    qc, kc, vc = to_chunks(q), to_chunks(k), to_chunks(v)
    gc = jnp.cumsum(to_chunks(g), axis=-1)
    bc = to_chunks(beta)
    kb = kc * bc[..., None]
    vb = vc * bc[..., None]
    tril = jnp.tril(jnp.ones((c, c), bool))
    strict = jnp.tril(jnp.ones((c, c), bool), -1)
    decay = jnp.exp(jnp.where(tril, gc[..., :, None] - gc[..., None, :], -jnp.inf))
    m = jnp.where(strict, jnp.einsum('nbhcd,nbhsd->nbhcs', kb, kc) * decay, 0.0)
    a_mat = m + jnp.eye(c, dtype=m.dtype)
    rhs = jnp.concatenate([vb, kb * jnp.exp(gc)[..., None]], axis=-1)
    sol = lax.linalg.triangular_solve(a_mat, rhs, left_side=True, lower=True, unit_diagonal=True)
    u, w = sol[..., :dv], sol[..., dv:]
    qk = jnp.einsum('nbhcd,nbhsd->nbhcs', qc, kc) * decay

    def step(s, xs):
        qi, ki, ui, wi, gi, qki = xs
        v_new = ui - jnp.einsum('bhcd,bhde->bhce', wi, s)
        o = (jnp.einsum('bhcd,bhde->bhce', qi * jnp.exp(gi)[..., None], s)
             + jnp.einsum('bhcs,bhse->bhce', qki, v_new))
        g_last = gi[..., -1]
        s = (s * jnp.exp(g_last)[..., None, None]
             + jnp.einsum('bhcd,bhce->bhde', ki * jnp.exp(g_last[..., None] - gi)[..., None], v_new))
        return s, o

    s_fin, o = lax.scan(step, s0, (qc, kc, u, w, gc, qk))
    o = jnp.transpose(o, (1, 0, 3, 2, 4)).reshape(bsz, n * c, h, dv)[:, :t]
    return o, s_fin


def gated_deltanet(qkv, z, b_raw, a_raw, conv_buf, s0, conv_w, a_log, dt_bias, norm_g):
    bsz, t, _ = qkv.shape
    dt = qkv.dtype
    f32 = jnp.float32
    qkv_c, new_buf = causal_dwconv(qkv, conv_buf, conv_w)
    qkv_c = jax.nn.silu(qkv_c.astype(f32))
    q, k, v = jnp.split(qkv_c, 3, axis=-1)
    q = l2norm(q.reshape(bsz, t, DN_HEADS, DN_DK)) * (DN_DK ** -0.5)
    k = l2norm(k.reshape(bsz, t, DN_HEADS, DN_DK))
    v = v.reshape(bsz, t, DN_HEADS, DN_DV)
    beta = jax.nn.sigmoid(b_raw.astype(f32))
    g = -jnp.exp(a_log.astype(f32)) * jax.nn.softplus(a_raw.astype(f32) + dt_bias.astype(f32))
    o, s_new = chunk_gated_delta(q, k, v, g, beta, s0.astype(f32))
    o = rmsnorm(o, norm_g) * jax.nn.silu(z.astype(f32).reshape(bsz, t, DN_HEADS, DN_DV))
    return o.reshape(bsz, t, GROUP_W).astype(dt), s_new.astype(s0.dtype), new_buf


def s5_mixer(u, h0_re, h0_im, a_re, a_im, log_dt, b_re, b_im, c_re, c_im, d_skip, w_glu, b_glu):
    bsz, t, _ = u.shape
    f32 = jnp.float32
    u32 = u.astype(f32)
    ug = u32.reshape(bsz, t, SSM_GROUPS, SSM_CH)
    ar, ai = a_re.astype(f32), a_im.astype(f32)
    step = jnp.exp(log_dt.astype(f32))[:, None]
    mag = jnp.exp(ar * step)
    ab_re, ab_im = mag * jnp.cos(ai * step), mag * jnp.sin(ai * step)
    den = ar * ar + ai * ai
    cr = ((ab_re - 1.0) * ar + ab_im * ai) / den
    ci = (ab_im * ar - (ab_re - 1.0) * ai) / den
    br, bi = b_re.astype(f32), b_im.astype(f32)
    bb_re = cr[..., None] * br - ci[..., None] * bi
    bb_im = cr[..., None] * bi + ci[..., None] * br
    x_re = jnp.einsum('btgc,gnc->btgn', ug, bb_re)
    x_im = jnp.einsum('btgc,gnc->btgn', ug, bb_im)
    h0r, h0i = h0_re.astype(f32), h0_im.astype(f32)
    x_re = x_re.at[:, 0].add(ab_re * h0r - ab_im * h0i)
    x_im = x_im.at[:, 0].add(ab_re * h0i + ab_im * h0r)
    a_seq_re = jnp.broadcast_to(ab_re, x_re.shape)
    a_seq_im = jnp.broadcast_to(ab_im, x_im.shape)
    _, _, h_re, h_im = lax.associative_scan(complex_combine, (a_seq_re, a_seq_im, x_re, x_im), axis=1)
    y = (jnp.einsum('gcn,btgn->btgc', c_re.astype(f32), h_re)
         - jnp.einsum('gcn,btgn->btgc', c_im.astype(f32), h_im)).reshape(bsz, t, GROUP_W)
    y = y + d_skip.astype(f32) * u32
    y = jax.nn.gelu(y)
    y = y * jax.nn.sigmoid(y @ w_glu.astype(f32) + b_glu.astype(f32))
    return y.astype(u.dtype), h_re[:, -1].astype(h0_re.dtype), h_im[:, -1].astype(h0_im.dtype)


def t5_bucket(dist):
    exact = REL_BUCKETS // 2
    d = jnp.maximum(dist.astype(jnp.float32), 1.0)
    large = exact + (jnp.log(d / exact) / math.log(REL_MAX_DIST / exact) * (REL_BUCKETS - exact)).astype(jnp.int32)
    large = jnp.minimum(large, REL_BUCKETS - 1)
    return jnp.where(dist < exact, dist, large)


def dilated_swa(q, k_new, v_new, k_buf, v_buf, p0, rel_bias):
    bsz, t, h, hd = q.shape
    lb = k_buf.shape[1]
    k_ext = jnp.concatenate([k_buf.astype(k_new.dtype), k_new], axis=1)
    v_ext = jnp.concatenate([v_buf.astype(v_new.dtype), v_new], axis=1)
    qb = min(SWA_QBLOCK, t)
    nb = -(-t // qb)
    qp = jnp.pad(q, ((0, 0), (0, nb * qb - t), (0, 0), (0, 0)))
    q_blocks = jnp.moveaxis(qp.reshape(bsz, nb, qb, h, hd), 1, 0)
    scale = hd ** -0.5
    groups = []
    for win, dil in DILATED:
        dist = jnp.arange(win // dil + 1, dtype=jnp.int32) * dil
        bias = rel_bias[t5_bucket(dist)].T.astype(jnp.float32)
        groups.append((dist, bias))

    def block_fn(args):
        ib, qi = args
        loc = ib * qb + jnp.arange(qb, dtype=jnp.int32)
        qi32 = qi.astype(jnp.float32)
        outs, maxs, sums = [], [], []
        for dist, bias in groups:
            rel = loc[:, None] - dist[None, :]
            valid = (p0 + rel) >= 0
            idx = jnp.clip(lb + rel, 0, lb + t - 1)
            kg = k_ext[:, idx].astype(jnp.float32)
            vg = v_ext[:, idx].astype(jnp.float32)
            logits = jnp.einsum('bqhd,bqshd->bhqs', qi32, kg) * scale + bias[None, :, None, :]
            logits = jnp.where(valid[None, None], logits, -jnp.inf)
            mx = jnp.max(logits, axis=-1, keepdims=True)
            e = jnp.exp(logits - mx)
            s = jnp.sum(e, axis=-1, keepdims=True)
            outs.append(jnp.einsum('bhqs,bqshd->bhqd', e, vg) / s)
            maxs.append(mx)
            sums.append(s)
        ms = jnp.stack(maxs)
        wts = jnp.stack(sums) * jnp.exp(ms - jnp.max(ms, axis=0, keepdims=True))
        o = jnp.sum(wts * jnp.stack(outs), axis=0) / jnp.sum(wts, axis=0)
        return jnp.transpose(o, (0, 2, 1, 3))

    o = lax.map(block_fn, (jnp.arange(nb, dtype=jnp.int32), q_blocks))
    o = jnp.moveaxis(o, 0, 1).reshape(bsz, nb * qb, h, hd)[:, :t]
    return o.astype(q.dtype)


def rglru_mixer(xb, gate, conv_buf, h0, conv_w, conv_b, w_a, b_a, w_x, b_x, lam):
    bsz, t, _ = xb.shape
    f32 = jnp.float32
    xc, new_buf = causal_dwconv(xb, conv_buf, conv_w)
    xc = (xc + conv_b).astype(f32)
    xr = xc.reshape(bsz, t, LRU_BLOCKS, LRU_BW)
    r = jax.nn.sigmoid(jnp.einsum('btkc,kcd->btkd', xr, w_a.astype(f32)).reshape(bsz, t, GROUP_W) + b_a.astype(f32))
    i = jax.nn.sigmoid(jnp.einsum('btkc,kcd->btkd', xr, w_x.astype(f32)).reshape(bsz, t, GROUP_W) + b_x.astype(f32))
    log_a = -LRU_C * r * jax.nn.softplus(-lam.astype(f32))
    a = jnp.exp(log_a)
    mult = jnp.sqrt(jnp.maximum(-jnp.expm1(2.0 * log_a), 0.0))
    bx = mult * i * xc
    bx = bx.at[:, 0].add(a[:, 0] * h0.astype(f32))
    _, hs = lax.associative_scan(linear_combine, (a, bx), axis=1)
    y = hs * jax.nn.gelu(gate.astype(f32))
    return y.astype(xb.dtype), hs[:, -1].astype(h0.dtype), new_buf


def memory_kv(mem, g_mem, w_k, w_v):
    bsz, m, _ = mem.shape
    mn = rmsnorm(mem, g_mem)
    return ((mn @ w_k).reshape(bsz, m, MEM_HEADS, MEM_HD), (mn @ w_v).reshape(bsz, m, MEM_HEADS, MEM_HD))


def memory_attn(xn, mem_k, mem_v, w_q, w_o):
    bsz, t, _ = xn.shape
    q = (xn @ w_q).reshape(bsz, t, MEM_HEADS, MEM_HD).astype(jnp.float32)
    logits = jnp.einsum('bthd,bmhd->bhtm', q, mem_k.astype(jnp.float32)) * (MEM_HD ** -0.5)
    p = jax.nn.softmax(logits, axis=-1)
    o = jnp.einsum('bhtm,bmhd->bthd', p, mem_v.astype(jnp.float32)).reshape(bsz, t, MEM_HEADS * MEM_HD)
    return o.astype(xn.dtype) @ w_o


def conv_ffn(xn, buf, w_up, conv_w, w_down):
    hup = xn @ w_up
    hc, new_buf = causal_dwconv(hup, buf, conv_w)
    u, g = jnp.split(hc, 2, axis=-1)
    return (jax.nn.silu(g) * u) @ w_down, new_buf


def fresh_state(bsz, dtype):
    return (jnp.zeros((bsz, 0, SWA_HEADS, SWA_HD), dtype), jnp.zeros((bsz, 0, SWA_HEADS, SWA_HD), dtype),
            jnp.zeros((bsz, DN_HEADS, DN_DK, DN_DV), jnp.float32), jnp.zeros((bsz, DN_CONV - 1, 3 * GROUP_W), dtype),
            jnp.zeros((bsz, SSM_GROUPS, SSM_STATE), jnp.float32), jnp.zeros((bsz, SSM_GROUPS, SSM_STATE), jnp.float32),
            jnp.zeros((bsz, GROUP_W), jnp.float32), jnp.zeros((bsz, LRU_CONV - 1, GROUP_W), dtype),
            jnp.zeros((bsz, FFN_CONV - 1, 2 * D_FF), dtype))


def trunk_layer(x, mem_k, mem_v, win_k, win_v, dn_state, dn_buf, ssm_re, ssm_im, lru_h, lru_buf, ffn_buf,
                p0, rel_bias, lp):
    bsz, t, _ = x.shape
    dt = x.dtype
    proj = rmsnorm(x, lp['g_mix']) @ lp['w_in']
    dn_qkv, dn_z, dn_b, dn_a, ssm_u, swa_qkv, lru_x, lru_g = jnp.split(proj, in_split_points(), axis=-1)
    o_a, dn_state_new, dn_buf_new = gated_deltanet(dn_qkv, dn_z, dn_b, dn_a, dn_buf, dn_state, lp['dn_conv_w'],
                                                   lp['dn_a_log'], lp['dn_dt_bias'], lp['dn_norm_g'])
    o_b, ssm_re_new, ssm_im_new = s5_mixer(ssm_u, ssm_re, ssm_im, lp['ssm_a_re'], lp['ssm_a_im'], lp['ssm_log_dt'],
                                           lp['ssm_b_re'], lp['ssm_b_im'], lp['ssm_c_re'], lp['ssm_c_im'],
                                           lp['ssm_d'], lp['ssm_w_glu'], lp['ssm_b_glu'])
    q, k, v = [zz.reshape(bsz, t, SWA_HEADS, SWA_HD) for zz in jnp.split(swa_qkv, 3, axis=-1)]
    o_c = dilated_swa(q, k, v, win_k, win_v, p0, rel_bias).reshape(bsz, t, GROUP_W)
    o_d, lru_h_new, lru_buf_new = rglru_mixer(lru_x, lru_g, lru_buf, lru_h, lp['lru_conv_w'], lp['lru_conv_b'],
                                              lp['lru_w_a'], lp['lru_b_a'], lp['lru_w_x'], lp['lru_b_x'], lp['lru_lam'])
    mix = jnp.concatenate([o_a, o_b, o_c, o_d], axis=-1).astype(dt)
    x = x + (mix @ lp['w_out']).astype(dt)
    x = x + memory_attn(rmsnorm(x, lp['g_cross']), mem_k, mem_v, lp['w_mem_q'], lp['w_mem_o']).astype(dt)
    f, ffn_buf_new = conv_ffn(rmsnorm(x, lp['g_ffn']), ffn_buf, lp['w_up'], lp['ffn_conv_w'], lp['w_down'])
    x = x + f.astype(dt)
    return x, k, v, dn_state_new, dn_buf_new, ssm_re_new, ssm_im_new, lru_h_new, lru_buf_new, ffn_buf_new


def setup_inputs(seed: int = 0) -> dict:
    key = jax.random.key(seed)
    ks = iter(jax.random.split(key, 80))
    f32 = jnp.float32

    def nrm(shape, scale=1.0):
        return jax.random.normal(next(ks), shape, f32) * scale

    def unif(shape, lo, hi):
        return jax.random.uniform(next(ks), shape, f32, lo, hi)

    def gain(shape):
        return 1.0 + nrm(shape, 0.01)

    win_buf = min(WIN_MAX, PAST_LEN)
    dn_dt = jnp.exp(unif((DEPTH, DN_HEADS), math.log(1e-3), math.log(1e-1)))
    lam_s = unif((DEPTH, GROUP_W), 0.9, 0.999) ** (1.0 / LRU_C)
    return {
        'x_prompt': nrm((BATCH, SEQ, D_MODEL)),
        'x_sample': nrm((DEC_BATCH, DEC_SEQ, D_MODEL)),
        'mem_prompt': nrm((BATCH, MEM_LEN, D_MODEL)),
        'cache_mem_k': nrm((DEPTH, DEC_BATCH, MEM_LEN, MEM_HEADS, MEM_HD)),
        'cache_mem_v': nrm((DEPTH, DEC_BATCH, MEM_LEN, MEM_HEADS, MEM_HD)),
        'cache_win_k': nrm((DEPTH, DEC_BATCH, win_buf, SWA_HEADS, SWA_HD)),
        'cache_win_v': nrm((DEPTH, DEC_BATCH, win_buf, SWA_HEADS, SWA_HD)),
        'state_delta': nrm((DEPTH, DEC_BATCH, DN_HEADS, DN_DK, DN_DV), 0.3),
        'state_delta_conv': nrm((DEPTH, DEC_BATCH, DN_CONV - 1, 3 * GROUP_W)),
        'state_ssm_re': nrm((DEPTH, DEC_BATCH, SSM_GROUPS, SSM_STATE), 0.5),
        'state_ssm_im': nrm((DEPTH, DEC_BATCH, SSM_GROUPS, SSM_STATE), 0.5),
        'state_lru': nrm((DEPTH, DEC_BATCH, GROUP_W), 0.5),
        'state_lru_conv': nrm((DEPTH, DEC_BATCH, LRU_CONV - 1, GROUP_W)),
        'state_ffn_conv': nrm((DEPTH, DEC_BATCH, FFN_CONV - 1, 2 * D_FF)),
        'rel_bias': nrm((REL_BUCKETS, SWA_HEADS), 0.1),
        'g_mix': gain((DEPTH, D_MODEL)),
        'w_in': nrm((DEPTH, D_MODEL, IN_W), D_MODEL ** -0.5),
        'dn_conv_w': nrm((DEPTH, DN_CONV, 3 * GROUP_W), 0.5),
        'dn_a_log': jnp.log(unif((DEPTH, DN_HEADS), 1.0, 16.0)),
        'dn_dt_bias': dn_dt + jnp.log(-jnp.expm1(-dn_dt)),
        'dn_norm_g': gain((DEPTH, DN_DV)),
        'ssm_a_re': -0.5 + nrm((DEPTH, SSM_GROUPS, SSM_STATE), 0.01),
        'ssm_a_im': math.pi * jnp.arange(SSM_STATE, dtype=f32) + nrm((DEPTH, SSM_GROUPS, SSM_STATE), 0.01),
        'ssm_log_dt': unif((DEPTH, SSM_GROUPS), math.log(1e-3), math.log(1e-1)),
        'ssm_b_re': nrm((DEPTH, SSM_GROUPS, SSM_STATE, SSM_CH), (2 * SSM_CH) ** -0.5),
        'ssm_b_im': nrm((DEPTH, SSM_GROUPS, SSM_STATE, SSM_CH), (2 * SSM_CH) ** -0.5),
        'ssm_c_re': nrm((DEPTH, SSM_GROUPS, SSM_CH, SSM_STATE), SSM_STATE ** -0.5),
        'ssm_c_im': nrm((DEPTH, SSM_GROUPS, SSM_CH, SSM_STATE), SSM_STATE ** -0.5),
        'ssm_d': nrm((DEPTH, GROUP_W)),
        'ssm_w_glu': nrm((DEPTH, GROUP_W, GROUP_W), GROUP_W ** -0.5),
        'ssm_b_glu': nrm((DEPTH, GROUP_W), 0.01),
        'lru_conv_w': nrm((DEPTH, LRU_CONV, GROUP_W), 0.5),
        'lru_conv_b': nrm((DEPTH, GROUP_W), 0.01),
        'lru_w_a': nrm((DEPTH, LRU_BLOCKS, LRU_BW, LRU_BW), LRU_BW ** -0.5),
        'lru_b_a': nrm((DEPTH, GROUP_W), 0.01),
        'lru_w_x': nrm((DEPTH, LRU_BLOCKS, LRU_BW, LRU_BW), LRU_BW ** -0.5),
        'lru_b_x': nrm((DEPTH, GROUP_W), 0.01),
        'lru_lam': jnp.log(lam_s) - jnp.log1p(-lam_s),
        'w_out': nrm((DEPTH, D_MIX, D_MODEL), D_MIX ** -0.5),
        'g_cross': gain((DEPTH, D_MODEL)),
        'g_mem': gain((DEPTH, D_MODEL)),
        'w_mem_q': nrm((DEPTH, D_MODEL, MEM_HEADS * MEM_HD), D_MODEL ** -0.5),
        'w_mem_k': nrm((DEPTH, D_MODEL, MEM_HEADS * MEM_HD), D_MODEL ** -0.5),
        'w_mem_v': nrm((DEPTH, D_MODEL, MEM_HEADS * MEM_HD), D_MODEL ** -0.5),
        'w_mem_o': nrm((DEPTH, MEM_HEADS * MEM_HD, D_MODEL), (MEM_HEADS * MEM_HD) ** -0.5),
        'g_ffn': gain((DEPTH, D_MODEL)),
        'w_up': nrm((DEPTH, D_MODEL, 2 * D_FF), D_MODEL ** -0.5),
        'ffn_conv_w': nrm((DEPTH, FFN_CONV, 2 * D_FF), FFN_CONV ** -0.5),
        'w_down': nrm((DEPTH, D_FF, D_MODEL), D_FF ** -0.5),
        'g_final': gain((D_MODEL,)),
    }


def reference(x_prompt, x_sample, mem_prompt, cache_mem_k, cache_mem_v, cache_win_k, cache_win_v,
              state_delta, state_delta_conv, state_ssm_re, state_ssm_im, state_lru, state_lru_conv,
              state_ffn_conv, rel_bias, g_mix, w_in, dn_conv_w, dn_a_log, dn_dt_bias, dn_norm_g,
              ssm_a_re, ssm_a_im, ssm_log_dt, ssm_b_re, ssm_b_im, ssm_c_re, ssm_c_im, ssm_d,
              ssm_w_glu, ssm_b_glu, lru_conv_w, lru_conv_b, lru_w_a, lru_b_a, lru_w_x, lru_b_x,
              lru_lam, w_out, g_cross, g_mem, w_mem_q, w_mem_k, w_mem_v, w_mem_o, g_ffn, w_up,
              ffn_conv_w, w_down, g_final):
    bp = x_prompt.shape[0]
    yp, ys = x_prompt, x_sample
    names = ('mem_k', 'mem_v', 'win_k', 'win_v', 'delta', 'delta_conv', 'ssm_re', 'ssm_im', 'lru', 'lru_conv', 'ffn_conv')
    P = {nm: [] for nm in names}
    S = {nm: [] for nm in names[2:]}
    for l in range(DEPTH):
        lp = {'g_mix': g_mix[l], 'w_in': w_in[l], 'dn_conv_w': dn_conv_w[l], 'dn_a_log': dn_a_log[l],
              'dn_dt_bias': dn_dt_bias[l], 'dn_norm_g': dn_norm_g[l], 'ssm_a_re': ssm_a_re[l],
              'ssm_a_im': ssm_a_im[l], 'ssm_log_dt': ssm_log_dt[l], 'ssm_b_re': ssm_b_re[l],
              'ssm_b_im': ssm_b_im[l], 'ssm_c_re': ssm_c_re[l], 'ssm_c_im': ssm_c_im[l], 'ssm_d': ssm_d[l],
              'ssm_w_glu': ssm_w_glu[l], 'ssm_b_glu': ssm_b_glu[l], 'lru_conv_w': lru_conv_w[l],
              'lru_conv_b': lru_conv_b[l], 'lru_w_a': lru_w_a[l], 'lru_b_a': lru_b_a[l], 'lru_w_x': lru_w_x[l],
              'lru_b_x': lru_b_x[l], 'lru_lam': lru_lam[l], 'w_out': w_out[l], 'g_cross': g_cross[l],
              'w_mem_q': w_mem_q[l], 'w_mem_o': w_mem_o[l], 'g_ffn': g_ffn[l], 'w_up': w_up[l],
              'ffn_conv_w': ffn_conv_w[l], 'w_down': w_down[l]}
        mk, mv = memory_kv(mem_prompt, g_mem[l], w_mem_k[l], w_mem_v[l])
        yp, kp, vp, dsp, dcp, srp, sip, lhp, lcp, fcp = trunk_layer(
            yp, mk, mv, *fresh_state(bp, x_prompt.dtype), 0, rel_bias, lp)
        keep = min(WIN_MAX, kp.shape[1])
        for nm, val in (('mem_k', mk), ('mem_v', mv), ('win_k', kp[:, kp.shape[1] - keep:]),
                        ('win_v', vp[:, vp.shape[1] - keep:]), ('delta', dsp), ('delta_conv', dcp),
                        ('ssm_re', srp), ('ssm_im', sip), ('lru', lhp), ('lru_conv', lcp), ('ffn_conv', fcp)):
            P[nm].append(val)
        ys, ks_, vs_, dss, dcs, srs, sis, lhs, lcs, fcs = trunk_layer(
            ys, cache_mem_k[l], cache_mem_v[l], cache_win_k[l], cache_win_v[l], state_delta[l],
            state_delta_conv[l], state_ssm_re[l], state_ssm_im[l], state_lru[l], state_lru_conv[l],
            state_ffn_conv[l], PAST_LEN, rel_bias, lp)
        for nm, val in (('win_k', ks_), ('win_v', vs_), ('delta', dss), ('delta_conv', dcs), ('ssm_re', srs),
                        ('ssm_im', sis), ('lru', lhs), ('lru_conv', lcs), ('ffn_conv', fcs)):
            S[nm].append(val)
    y_prompt = rmsnorm(yp, g_final)
    y_sample = rmsnorm(ys, g_final)
    p_mem_k = jnp.stack(P['mem_k'])
    p_mem_v = jnp.stack(P['mem_v'])
    p_win_k = jnp.stack(P['win_k'])
    p_win_v = jnp.stack(P['win_v'])
    p_delta = jnp.stack(P['delta'])
    p_delta_conv = jnp.stack(P['delta_conv'])
    p_ssm_re = jnp.stack(P['ssm_re'])
    p_ssm_im = jnp.stack(P['ssm_im'])
    p_lru = jnp.stack(P['lru'])
    p_lru_conv = jnp.stack(P['lru_conv'])
    p_ffn_conv = jnp.stack(P['ffn_conv'])
    s_win_k = jnp.stack(S['win_k'])
    s_win_v = jnp.stack(S['win_v'])
    s_delta = jnp.stack(S['delta'])
    s_delta_conv = jnp.stack(S['delta_conv'])
    s_ssm_re = jnp.stack(S['ssm_re'])
    s_ssm_im = jnp.stack(S['ssm_im'])
    s_lru = jnp.stack(S['lru'])
    s_lru_conv = jnp.stack(S['lru_conv'])
    s_ffn_conv = jnp.stack(S['ffn_conv'])
    return (y_prompt, y_sample, p_mem_k, p_mem_v, p_win_k, p_win_v, p_delta, p_delta_conv, p_ssm_re, p_ssm_im,
            p_lru, p_lru_conv, p_ffn_conv, s_win_k, s_win_v, s_delta, s_delta_conv, s_ssm_re, s_ssm_im, s_lru,
            s_lru_conv, s_ffn_conv)
```

```python
import functools
import math

import numpy as np
import jax
import jax.numpy as jnp
from jax import lax
from jax.experimental import pallas as pl
from jax.experimental.pallas import tpu as pltpu

F32 = jnp.float32
BF16 = jnp.bfloat16
HIGHEST = lax.Precision.HIGHEST

D_MODEL = 2048
DEPTH = 4
PAST_LEN = 16384
GROUP_W = 512
DN_HEADS = 4
DN_DK = 128
DN_CHUNK = 64
SSM_CH = 16
SSM_GROUPS = 32
SSM_STATE = 64
SSM_W = SSM_GROUPS * SSM_STATE
SWA_HEADS = 8
SWA_HD = 64
DILATED = ((128, 1), (512, 4), (2048, 16))
WIN_MAX = 2048
SWA_BLOCK = 128
REL_BUCKETS = 32
REL_MAX_DIST = 2048
LRU_BLOCKS = 8
LRU_BW = 64
LRU_C = 8.0
MEM_LEN = 256
MEM_HEADS = 4
MEM_HD = 128
D_FF = 5632
EPS = 1e-6
NEG = -1e30

COL_DN_QKV = 0
COL_DN_Z = 1536
COL_SSM_U = 2048
COL_SWA_Q = 2560
COL_SWA_K = 3072
COL_SWA_V = 3584
COL_LRU_X = 4096
COL_LRU_G = 4608
COL_DN_BG = 5120
IN_W = 5128
IN_W_PAD = 5376
IN_TN = 768

VMEM_LIMIT = 56 * 1024 * 1024


def _cparams(n_axes):
    return pltpu.CompilerParams(dimension_semantics=("arbitrary",) * n_axes,
                                vmem_limit_bytes=VMEM_LIMIT)


def _rms(x, g):
    return x * lax.rsqrt(jnp.mean(x * x, axis=-1, keepdims=True) + EPS) * g


def _sigmoid(x):
    return jax.nn.sigmoid(x)


def _softplus(x):
    return jnp.maximum(x, 0.0) + jnp.log1p(jnp.exp(-jnp.abs(x)))


def _gelu(x):
    return 0.5 * x * (1.0 + jnp.tanh(math.sqrt(2.0 / math.pi) * (x + 0.044715 * (x * x * x))))


def _dot(a, b, prec=None):
    return jnp.dot(a, b, preferred_element_type=F32, precision=prec)


def _dot_nt(a, b, prec=None):
    return lax.dot_general(a, b, (((1,), (1,)), ((), ())), preferred_element_type=F32, precision=prec)


def _dot_tn(a, b, prec=None):
    return lax.dot_general(a, b, (((0,), (0,)), ((), ())), preferred_element_type=F32, precision=prec)


def _shift_rows(x, s, fill):
    rows = lax.broadcasted_iota(jnp.int32, x.shape, 0)
    return jnp.where(rows >= s, pltpu.roll(x, s, 0), fill)


def _nmm_kernel(x_ref, g_ref, w_ref, o_ref, xn_ref):
    @pl.when(pl.program_id(1) == 0)
    def _():
        xn_ref[...] = _rms(x_ref[...], g_ref[...]).astype(BF16)

    o_ref[...] = _dot(xn_ref[...], w_ref[...])


def norm_matmul(x, g, w, tm, tn):
    m, k = x.shape
    n = w.shape[1]
    return pl.pallas_call(
        _nmm_kernel,
        grid=(m // tm, n // tn),
        in_specs=[pl.BlockSpec((tm, k), lambda i, j: (i, 0)),
                  pl.BlockSpec((1, k), lambda i, j: (0, 0)),
                  pl.BlockSpec((k, tn), lambda i, j: (0, j))],
        out_specs=pl.BlockSpec((tm, tn), lambda i, j: (i, j)),
        out_shape=jax.ShapeDtypeStruct((m, n), F32),
        scratch_shapes=[pltpu.VMEM((tm, k), BF16)],
        compiler_params=_cparams(2),
        name="norm_matmul",
    )(x, g.reshape(1, k), w)


def _rmsnorm_kernel(x_ref, g_ref, o_ref):
    o_ref[...] = _rms(x_ref[...], g_ref[...])


def rmsnorm_rows(x, g, tm):
    m, k = x.shape
    return pl.pallas_call(
        _rmsnorm_kernel,
        grid=(m // tm,),
        in_specs=[pl.BlockSpec((tm, k), lambda i: (i, 0)), pl.BlockSpec((1, k), lambda i: (0, 0))],
        out_specs=pl.BlockSpec((tm, k), lambda i: (i, 0)),
        out_shape=jax.ShapeDtypeStruct((m, k), F32),
        compiler_params=_cparams(1),
        name="final_rmsnorm",
    )(x, g.reshape(1, k))


def _mix_out_kernel(a_ref, b_ref, c_ref, d_ref, w_ref, r_ref, o_ref):
    acc = r_ref[...]
    for i, part in enumerate((a_ref, b_ref, c_ref, d_ref)):
        acc = acc + _dot(part[...].astype(BF16), w_ref[i * GROUP_W:(i + 1) * GROUP_W, :])
    o_ref[...] = acc


def mix_out(parts, w, res, tm, tn):
    m, n = res.shape
    part_spec = pl.BlockSpec((tm, GROUP_W), lambda i, j: (i, 0))
    return pl.pallas_call(
        _mix_out_kernel,
        grid=(m // tm, n // tn),
        in_specs=[part_spec] * 4 + [pl.BlockSpec((4 * GROUP_W, tn), lambda i, j: (0, j)),
                                    pl.BlockSpec((tm, tn), lambda i, j: (i, j))],
        out_specs=pl.BlockSpec((tm, tn), lambda i, j: (i, j)),
        out_shape=jax.ShapeDtypeStruct((m, n), F32),
        compiler_params=_cparams(2),
        name="mix_out",
    )(*parts, w, res)


def _dn_kernel(qkv_ref, z_ref, bg_ref, buf_ref, s0_ref, cw_ref, alog_ref, dtb_ref, ng_ref,
               o_ref, s_ref, cbuf_ref, xx_ref, *, chunk):
    c = chunk
    n = pl.program_id(1)

    @pl.when(n == 0)
    def _():
        xx_ref[0:8, :] = buf_ref[0]
        s_ref[0] = s0_ref[0]

    xx_ref[8:8 + c, :] = qkv_ref[0]
    cw = cw_ref[...]
    y = (xx_ref[5:5 + c, :] * cw[0:1] + xx_ref[6:6 + c, :] * cw[1:2]
         + xx_ref[7:7 + c, :] * cw[2:3] + xx_ref[8:8 + c, :] * cw[3:4])
    y = y * _sigmoid(y)
    tail = xx_ref[c:c + 8, :]
    xx_ref[0:8, :] = tail
    cbuf_ref[0] = tail[5:8]

    bg = bg_ref[0]
    beta_all = _sigmoid(bg)
    g_all = -jnp.exp(alog_ref[...]) * _softplus(bg + dtb_ref[...])

    ri = lax.broadcasted_iota(jnp.int32, (c, c), 0)
    ci = lax.broadcasted_iota(jnp.int32, (c, c), 1)
    tril = ri >= ci
    strict = ri > ci
    m1 = tril.astype(F32)
    m2 = strict.astype(F32)
    eye = (ri == ci).astype(F32)
    gc_all = _dot(m1, g_all, HIGHEST)
    eg_all = jnp.exp(gc_all)
    el_all = jnp.exp(gc_all[c - 1:c, :] - gc_all)

    for h in range(DN_HEADS):
        lane = DN_HEADS + h
        q = y[:, h * DN_DK:(h + 1) * DN_DK]
        k = y[:, GROUP_W + h * DN_DK:GROUP_W + (h + 1) * DN_DK]
        v = y[:, 2 * GROUP_W + h * DN_DK:2 * GROUP_W + (h + 1) * DN_DK]
        q = q * lax.rsqrt(jnp.sum(q * q, axis=-1, keepdims=True) + EPS) * (DN_DK ** -0.5)
        k = k * lax.rsqrt(jnp.sum(k * k, axis=-1, keepdims=True) + EPS)
        beta = beta_all[:, h:h + 1]
        g_col = g_all[:, lane:lane + 1]
        eg = eg_all[:, lane:lane + 1]
        el = el_all[:, lane:lane + 1]
        eg_last = eg_all[c - 1:c, lane:lane + 1]
        dlog = _dot(m1, g_col * m2, HIGHEST)
        decay = jnp.where(tril, jnp.exp(jnp.where(tril, dlog, 0.0)), 0.0)
        kb = k * beta
        vb = v * beta
        mneg = -jnp.where(strict, _dot_nt(kb.astype(BF16), k.astype(BF16)) * decay, 0.0)
        tinv = eye + mneg
        p = mneg
        for _ in range(int(math.log2(c)) - 1):
            p = _dot(p, p, HIGHEST)
            tinv = tinv + _dot(tinv, p, HIGHEST)
        rhs = jnp.concatenate([vb, kb * eg], axis=-1)
        sol = _dot(tinv, rhs, HIGHEST)
        u = sol[:, :DN_DK]
        w = sol[:, DN_DK:]
        qk = _dot_nt(q.astype(BF16), k.astype(BF16)) * decay
        s = s_ref[0, h]
        sb = s.astype(BF16)
        v_new = u - _dot(w.astype(BF16), sb)
        o = _dot((q * eg).astype(BF16), sb) + _dot(qk.astype(BF16), v_new.astype(BF16))
        s_ref[0, h] = s * eg_last + _dot_tn((k * el).astype(BF16), v_new.astype(BF16))
        zh = z_ref[0, :, h * DN_DK:(h + 1) * DN_DK]
        o_ref[0, :, h * DN_DK:(h + 1) * DN_DK] = _rms(o, ng_ref[...]) * (zh * _sigmoid(zh))


def deltanet(proj, buf8, s0, conv_w, alog_row, dtb_row, norm_g, chunk):
    b, t, _ = proj.shape
    kern = functools.partial(_dn_kernel, chunk=chunk)
    w3 = 3 * GROUP_W
    return pl.pallas_call(
        kern,
        grid=(b, t // chunk),
        in_specs=[pl.BlockSpec((1, chunk, w3), lambda bi, n: (bi, n, COL_DN_QKV // w3)),
                  pl.BlockSpec((1, chunk, GROUP_W), lambda bi, n: (bi, n, COL_DN_Z // GROUP_W)),
                  pl.BlockSpec((1, chunk, 128), lambda bi, n: (bi, n, COL_DN_BG // 128)),
                  pl.BlockSpec((1, 8, w3), lambda bi, n: (bi, 0, 0)),
                  pl.BlockSpec((1, DN_HEADS, DN_DK, DN_DK), lambda bi, n: (bi, 0, 0, 0)),
                  pl.BlockSpec((4, w3), lambda bi, n: (0, 0)),
                  pl.BlockSpec((1, 128), lambda bi, n: (0, 0)),
                  pl.BlockSpec((1, 128), lambda bi, n: (0, 0)),
                  pl.BlockSpec((1, DN_DK), lambda bi, n: (0, 0))],
        out_specs=[pl.BlockSpec((1, chunk, GROUP_W), lambda bi, n: (bi, n, 0)),
                   pl.BlockSpec((1, DN_HEADS, DN_DK, DN_DK), lambda bi, n: (bi, 0, 0, 0)),
                   pl.BlockSpec((1, 3, w3), lambda bi, n: (bi, 0, 0))],
        out_shape=[jax.ShapeDtypeStruct((b, t, GROUP_W), F32),
                   jax.ShapeDtypeStruct((b, DN_HEADS, DN_DK, DN_DK), F32),
                   jax.ShapeDtypeStruct((b, 3, w3), F32)],
        scratch_shapes=[pltpu.VMEM((8 + chunk, w3), F32)],
        compiler_params=_cparams(2),
        name="deltanet",
    )(proj, proj, proj, buf8, s0, conv_w, alog_row, dtb_row, norm_g.reshape(1, DN_DK))


def _s5_kernel(u_ref, h0r_ref, h0i_ref, apr_ref, api_ref, bre_ref, bim_ref, cre_ref, cim_ref,
               d_ref, wg_ref, bgl_ref, y_ref, hr_ref, hi_ref, xr_ref, xi_ref, *, ts):
    @pl.when(pl.program_id(1) == 0)
    def _():
        hr_ref[0] = h0r_ref[0]
        hi_ref[0] = h0i_ref[0]

    u = u_ref[0]
    ub = u.astype(BF16)
    xr_ref[...] = _dot(ub, bre_ref[...])
    xi_ref[...] = _dot(ub, bim_ref[...])
    apr = apr_ref[...]
    api = api_ref[...]

    def group(gi, carry):
        cr, ci = carry
        r0 = pl.multiple_of(gi * 8, 8)
        xr = xr_ref[pl.ds(r0, 8), :]
        xi = xi_ref[pl.ds(r0, 8), :]
        for s in (1, 2, 4):
            ar = apr[s - 1:s]
            ai = api[s - 1:s]
            sr = _shift_rows(xr, s, 0.0)
            si = _shift_rows(xi, s, 0.0)
            xr, xi = xr + ar * sr - ai * si, xi + ar * si + ai * sr
        xr, xi = xr + apr * cr - api * ci, xi + apr * ci + api * cr
        xr_ref[pl.ds(r0, 8), :] = xr
        xi_ref[pl.ds(r0, 8), :] = xi
        return xr[7:8], xi[7:8]

    cr, ci = lax.fori_loop(0, ts // 8, group, (hr_ref[0], hi_ref[0]))
    hr_ref[0] = cr
    hi_ref[0] = ci
    y = _dot(xr_ref[...].astype(BF16), cre_ref[...]) - _dot(xi_ref[...].astype(BF16), cim_ref[...])
    y = _gelu(y + d_ref[...] * u)
    y_ref[0] = y * _sigmoid(_dot(y.astype(BF16), wg_ref[...]) + bgl_ref[...])


def s5_mixer(proj, h0r, h0i, apr, api, bre, bim, cre, cim, d_skip, w_glu, b_glu, ts):
    b, t, _ = proj.shape
    kern = functools.partial(_s5_kernel, ts=ts)
    full = lambda shape: pl.BlockSpec(shape, lambda bi, i: (0,) * len(shape))
    st_spec = pl.BlockSpec((1, 1, SSM_W), lambda bi, i: (bi, 0, 0))
    return pl.pallas_call(
        kern,
        grid=(b, t // ts),
        in_specs=[pl.BlockSpec((1, ts, GROUP_W), lambda bi, i: (bi, i, COL_SSM_U // GROUP_W)),
                  st_spec, st_spec,
                  full((8, SSM_W)), full((8, SSM_W)),
                  full((GROUP_W, SSM_W)), full((GROUP_W, SSM_W)),
                  full((SSM_W, GROUP_W)), full((SSM_W, GROUP_W)),
                  full((1, GROUP_W)), full((GROUP_W, GROUP_W)), full((1, GROUP_W))],
        out_specs=[pl.BlockSpec((1, ts, GROUP_W), lambda bi, i: (bi, i, 0)), st_spec, st_spec],
        out_shape=[jax.ShapeDtypeStruct((b, t, GROUP_W), F32),
                   jax.ShapeDtypeStruct((b, 1, SSM_W), F32),
                   jax.ShapeDtypeStruct((b, 1, SSM_W), F32)],
        scratch_shapes=[pltpu.VMEM((ts, SSM_W), F32), pltpu.VMEM((ts, SSM_W), F32)],
        compiler_params=_cparams(2),
        name="s5_mixer",
    )(proj, h0r, h0i, apr, api, bre, bim, cre, cim, d_skip.reshape(1, GROUP_W), w_glu,
      b_glu.reshape(1, GROUP_W))


def _swa_prompt_kernel(q_ref, k_ref, v_ref, bias_ref, o_ref):
    qi = pl.program_id(2)
    q2 = q_ref[0] * (SWA_HD ** -0.5)
    outs = []
    for hh in range(2):
        lo, hi = hh * SWA_HD, (hh + 1) * SWA_HD
        qh = q2[:, lo:hi].astype(BF16)

        def block(t, carry, hh=hh, lo=lo, hi=hi, qh=qh):
            m, l, acc = carry
            k0 = pl.multiple_of((qi - t) * SWA_BLOCK, SWA_BLOCK)
            kh = k_ref[0, pl.ds(k0, SWA_BLOCK), lo:hi].astype(BF16)
            vh = v_ref[0, pl.ds(k0, SWA_BLOCK), lo:hi].astype(BF16)
            s = _dot_nt(qh, kh) + bias_ref[hh, t]
            m_new = jnp.maximum(m, jnp.max(s, axis=-1, keepdims=True))
            alpha = jnp.exp(m - m_new)
            p = jnp.exp(s - m_new)
            l = alpha * l + jnp.sum(p, axis=-1, keepdims=True)
            acc = alpha * acc + _dot(p.astype(BF16), vh)
            return m_new, l, acc

        nblk = jnp.minimum(qi, WIN_MAX // SWA_BLOCK) + 1
        init = (jnp.full((SWA_BLOCK, 1), NEG, F32), jnp.zeros((SWA_BLOCK, 1), F32),
                jnp.zeros((SWA_BLOCK, SWA_HD), F32))
        _, l, acc = lax.fori_loop(0, nblk, block, init)
        outs.append(acc / l)
    o_ref[0] = jnp.concatenate(outs, axis=-1)


def swa_prompt(proj, bias):
    b, t, _ = proj.shape
    nq = t // SWA_BLOCK
    noff = bias.shape[1]
    qcol, kcol, vcol = COL_SWA_Q // 128, COL_SWA_K // 128, COL_SWA_V // 128
    return pl.pallas_call(
        _swa_prompt_kernel,
        grid=(b, SWA_HEADS // 2, nq),
        in_specs=[pl.BlockSpec((1, SWA_BLOCK, 128), lambda bi, hp, qi: (bi, qi, qcol + hp)),
                  pl.BlockSpec((1, t, 128), lambda bi, hp, qi: (bi, 0, kcol + hp)),
                  pl.BlockSpec((1, t, 128), lambda bi, hp, qi: (bi, 0, vcol + hp)),
                  pl.BlockSpec((2, noff, SWA_BLOCK, SWA_BLOCK), lambda bi, hp, qi: (hp, 0, 0, 0))],
        out_specs=pl.BlockSpec((1, SWA_BLOCK, 128), lambda bi, hp, qi: (bi, qi, hp)),
        out_shape=jax.ShapeDtypeStruct((b, t, GROUP_W), F32),
        compiler_params=_cparams(3),
        name="swa_prompt",
    )(proj, proj, proj, bias)


def _swa_sample_kernel(q_ref, kn_ref, vn_ref, kc_ref, vc_ref, bc_ref, bn_ref, o_ref):
    q = q_ref[0] * (SWA_HD ** -0.5)
    kn = kn_ref[0]
    vn = vn_ref[0]
    for h in range(SWA_HEADS):
        lo, hi = h * SWA_HD, (h + 1) * SWA_HD
        qh = q[:, lo:hi].astype(BF16)
        sc = _dot_nt(qh, kc_ref[0, :, lo:hi].astype(BF16)) + bc_ref[h]
        sn = _dot_nt(qh, kn[:, lo:hi].astype(BF16)) + bn_ref[h]
        m = jnp.maximum(jnp.max(sc, axis=-1, keepdims=True), jnp.max(sn, axis=-1, keepdims=True))
        pc = jnp.exp(sc - m)
        pn = jnp.exp(sn - m)
        l = jnp.sum(pc, axis=-1, keepdims=True) + jnp.sum(pn, axis=-1, keepdims=True)
        acc = (_dot(pc.astype(BF16), vc_ref[0, :, lo:hi].astype(BF16))
               + _dot(pn.astype(BF16), vn[:, lo:hi].astype(BF16)))
        o_ref[0, :, lo:hi] = acc / l


def swa_sample(proj, cache_k, cache_v, bias_c, bias_n):
    b, t, _ = proj.shape
    lb = cache_k.shape[1]
    new_spec = lambda col: pl.BlockSpec((1, t, GROUP_W), lambda bi: (bi, 0, col // GROUP_W))
    cache_spec = pl.BlockSpec((1, lb, GROUP_W), lambda bi: (bi, 0, 0))
    return pl.pallas_call(
        _swa_sample_kernel,
        grid=(b,),
        in_specs=[new_spec(COL_SWA_Q), new_spec(COL_SWA_K), new_spec(COL_SWA_V), cache_spec, cache_spec,
                  pl.BlockSpec((SWA_HEADS, t, lb), lambda bi: (0, 0, 0)),
                  pl.BlockSpec((SWA_HEADS, t, t), lambda bi: (0, 0, 0))],
        out_specs=pl.BlockSpec((1, t, GROUP_W), lambda bi: (bi, 0, 0)),
        out_shape=jax.ShapeDtypeStruct((b, t, GROUP_W), F32),
        compiler_params=_cparams(1),
        name="swa_sample",
    )(proj, proj, proj, cache_k, cache_v, bias_c, bias_n)


def _lru_kernel(x_ref, gate_ref, buf_ref, h0_ref, cw_ref, cb_ref, wa_ref, ba_ref, wx_ref, bx_ref, lam_ref,
                y_ref, h_ref, cbuf_ref, xx_ref, a_ref, b_ref, *, ts):
    @pl.when(pl.program_id(1) == 0)
    def _():
        xx_ref[0:8, :] = buf_ref[0]
        h_ref[0] = h0_ref[0]

    xx_ref[8:8 + ts, :] = x_ref[0]
    cw = cw_ref[...]
    xc = (xx_ref[5:5 + ts, :] * cw[0:1] + xx_ref[6:6 + ts, :] * cw[1:2]
          + xx_ref[7:7 + ts, :] * cw[2:3] + xx_ref[8:8 + ts, :] * cw[3:4]) + cb_ref[...]
    tail = xx_ref[ts:ts + 8, :]
    xx_ref[0:8, :] = tail
    cbuf_ref[0] = tail[5:8]

    xcb = xc.astype(BF16)
    r = _sigmoid(_dot(xcb, wa_ref[...]) + ba_ref[...])
    ig = _sigmoid(_dot(xcb, wx_ref[...]) + bx_ref[...])
    log_a = -LRU_C * r * _softplus(-lam_ref[...])
    a = jnp.exp(log_a)
    mult = jnp.sqrt(jnp.maximum(-jnp.tanh(log_a) * (a * a + 1.0), 0.0))
    a_ref[...] = a
    b_ref[...] = mult * ig * xc

    def group(gi, carry):
        r0 = pl.multiple_of(gi * 8, 8)
        av = a_ref[pl.ds(r0, 8), :]
        bv = b_ref[pl.ds(r0, 8), :]
        for s in (1, 2, 4):
            a_s = _shift_rows(av, s, 1.0)
            b_s = _shift_rows(bv, s, 0.0)
            bv = av * b_s + bv
            av = av * a_s
        hv = bv + av * carry
        b_ref[pl.ds(r0, 8), :] = hv
        return hv[7:8]

    h_ref[0] = lax.fori_loop(0, ts // 8, group, h_ref[0])
    y_ref[0] = b_ref[...] * _gelu(gate_ref[0])


def rglru(proj, buf8, h0, conv_w, conv_b, wa, ba, wx, bx, lam, ts):
    b, t, _ = proj.shape
    kern = functools.partial(_lru_kernel, ts=ts)
    full = lambda shape: pl.BlockSpec(shape, lambda bi, i: (0,) * len(shape))
    row = lambda v: v.reshape(1, GROUP_W)
    return pl.pallas_call(
        kern,
        grid=(b, t // ts),
        in_specs=[pl.BlockSpec((1, ts, GROUP_W), lambda bi, i: (bi, i, COL_LRU_X // GROUP_W)),
                  pl.BlockSpec((1, ts, GROUP_W), lambda bi, i: (bi, i, COL_LRU_G // GROUP_W)),
                  pl.BlockSpec((1, 8, GROUP_W), lambda bi, i: (bi, 0, 0)),
                  pl.BlockSpec((1, 1, GROUP_W), lambda bi, i: (bi, 0, 0)),
                  full((4, GROUP_W)), full((1, GROUP_W)),
                  full((GROUP_W, GROUP_W)), full((1, GROUP_W)),
                  full((GROUP_W, GROUP_W)), full((1, GROUP_W)), full((1, GROUP_W))],
        out_specs=[pl.BlockSpec((1, ts, GROUP_W), lambda bi, i: (bi, i, 0)),
                   pl.BlockSpec((1, 1, GROUP_W), lambda bi, i: (bi, 0, 0)),
                   pl.BlockSpec((1, 3, GROUP_W), lambda bi, i: (bi, 0, 0))],
        out_shape=[jax.ShapeDtypeStruct((b, t, GROUP_W), F32),
                   jax.ShapeDtypeStruct((b, 1, GROUP_W), F32),
                   jax.ShapeDtypeStruct((b, 3, GROUP_W), F32)],
        scratch_shapes=[pltpu.VMEM((8 + ts, GROUP_W), F32), pltpu.VMEM((ts, GROUP_W), F32),
                        pltpu.VMEM((ts, GROUP_W), F32)],
        compiler_params=_cparams(2),
        name="rglru",
    )(proj, proj, buf8, h0, conv_w, row(conv_b), wa, row(ba), wx, row(bx), row(lam))


def _cross_kernel(x_ref, g_ref, wq_ref, k_ref, v_ref, wo_ref, o_ref):
    x = x_ref[0]
    xn = _rms(x, g_ref[...]).astype(BF16)
    q = _dot(xn, wq_ref[...]) * (MEM_HD ** -0.5)
    outs = []
    for h in range(MEM_HEADS):
        lo, hi = h * MEM_HD, (h + 1) * MEM_HD
        s = _dot_nt(q[:, lo:hi].astype(BF16), k_ref[0, :, lo:hi].astype(BF16))
        m = jnp.max(s, axis=-1, keepdims=True)
        p = jnp.exp(s - m)
        l = jnp.sum(p, axis=-1, keepdims=True)
        outs.append(_dot((p / l).astype(BF16), v_ref[0, :, lo:hi].astype(BF16)))
    o = jnp.concatenate(outs, axis=-1)
    o_ref[0] = x + _dot(o.astype(BF16), wo_ref[...])


def cross_attn(x, g, wq, mem_k, mem_v, wo, tm):
    b, t, d = x.shape
    hw = MEM_HEADS * MEM_HD
    full = lambda shape: pl.BlockSpec(shape, lambda bi, i: (0,) * len(shape))
    return pl.pallas_call(
        _cross_kernel,
        grid=(b, t // tm),
        in_specs=[pl.BlockSpec((1, tm, d), lambda bi, i: (bi, i, 0)),
                  full((1, d)), full((d, hw)),
                  pl.BlockSpec((1, MEM_LEN, hw), lambda bi, i: (bi, 0, 0)),
                  pl.BlockSpec((1, MEM_LEN, hw), lambda bi, i: (bi, 0, 0)),
                  full((hw, d))],
        out_specs=pl.BlockSpec((1, tm, d), lambda bi, i: (bi, i, 0)),
        out_shape=jax.ShapeDtypeStruct((b, t, d), F32),
        compiler_params=_cparams(2),
        name="cross_attn",
    )(x, g.reshape(1, d), wq, mem_k, mem_v, wo)


def _ffn_kernel(x_ref, g_ref, wu_ref, wg_ref, cwu_ref, cwg_ref, wd_ref, bufu_ref, bufg_ref,
                o_ref, stu_ref, stg_ref, xn_ref, acc_ref, halo_u, halo_g, work_u, work_g, *, tm, nj):
    i = pl.program_id(1)
    j = pl.program_id(2)

    @pl.when(j == 0)
    def _():
        xn_ref[...] = _rms(x_ref[0], g_ref[...]).astype(BF16)

    @pl.when(i == 0)
    def _():
        halo_u[j] = bufu_ref[0]
        halo_g[j] = bufg_ref[0]

    def conv_half(w_ref, cw_ref, halo, work, st_ref):
        hup = _dot(xn_ref[...], w_ref[...])
        work[0:8, :] = halo[j]
        work[8:8 + tm, :] = hup
        cw = cw_ref[...]
        hc = work[6:6 + tm, :] * cw[0:1] + work[7:7 + tm, :] * cw[1:2] + hup * cw[2:3]
        tail = work[tm:tm + 8, :]
        halo[j] = tail
        st_ref[0, 0] = tail
        return hc

    cu = conv_half(wu_ref, cwu_ref, halo_u, work_u, stu_ref)
    cg = conv_half(wg_ref, cwg_ref, halo_g, work_g, stg_ref)
    act = (cg * _sigmoid(cg)) * cu
    part = _dot(act.astype(BF16), wd_ref[...])

    @pl.when(j == 0)
    def _():
        acc_ref[...] = part

    @pl.when(j > 0)
    def _():
        acc_ref[...] += part

    @pl.when(j == nj - 1)
    def _():
        o_ref[0] = x_ref[0] + acc_ref[...]


def conv_ffn(x, g, w_up, conv_w, w_down, buf8, tm, tf):
    b, t, d = x.shape
    nj = D_FF // tf
    kern = functools.partial(_ffn_kernel, tm=tm, nj=nj)
    return pl.pallas_call(
        kern,
        grid=(b, t // tm, nj),
        in_specs=[pl.BlockSpec((1, tm, d), lambda bi, i, j: (bi, i, 0)),
                  pl.BlockSpec((1, d), lambda bi, i, j: (0, 0)),
                  pl.BlockSpec((d, tf), lambda bi, i, j: (0, j)),
                  pl.BlockSpec((d, tf), lambda bi, i, j: (0, j + nj)),
                  pl.BlockSpec((3, tf), lambda bi, i, j: (0, j)),
                  pl.BlockSpec((3, tf), lambda bi, i, j: (0, j + nj)),
                  pl.BlockSpec((tf, d), lambda bi, i, j: (j, 0)),
                  pl.BlockSpec((1, 8, tf), lambda bi, i, j: (bi, 0, j)),
                  pl.BlockSpec((1, 8, tf), lambda bi, i, j: (bi, 0, j + nj))],
        out_specs=[pl.BlockSpec((1, tm, d), lambda bi, i, j: (bi, i, 0)),
                   pl.BlockSpec((1, 1, 8, tf), lambda bi, i, j: (bi, i, 0, j)),
                   pl.BlockSpec((1, 1, 8, tf), lambda bi, i, j: (bi, i, 0, j))],
        out_shape=[jax.ShapeDtypeStruct((b, t, d), F32),
                   jax.ShapeDtypeStruct((b, t // tm, 8, D_FF), F32),
                   jax.ShapeDtypeStruct((b, t // tm, 8, D_FF), F32)],
        scratch_shapes=[pltpu.VMEM((tm, d), BF16), pltpu.VMEM((tm, d), F32),
                        pltpu.VMEM((nj, 8, tf), F32), pltpu.VMEM((nj, 8, tf), F32),
                        pltpu.VMEM((8 + tm, tf), F32), pltpu.VMEM((8 + tm, tf), F32)],
        compiler_params=_cparams(3),
        name="conv_ffn",
    )(x, g.reshape(1, d), w_up, w_up, conv_w, conv_w, w_down, buf8, buf8)


def _pad_rows8(buf):
    return jnp.pad(buf, ((0, 0), (8 - buf.shape[1], 0), (0, 0)))


def _block_diag(blocks):
    g, r, c = blocks.shape
    eye = jnp.eye(g, dtype=blocks.dtype)
    return (eye[:, None, :, None] * blocks[:, :, None, :]).reshape(g * r, g * c)


def _t5_bucket(dist):
    exact = REL_BUCKETS // 2
    d = jnp.maximum(dist.astype(F32), 1.0)
    large = exact + (jnp.log(d / exact) / math.log(REL_MAX_DIST / exact) * (REL_BUCKETS - exact)).astype(jnp.int32)
    large = jnp.minimum(large, REL_BUCKETS - 1)
    return jnp.where(dist < exact, dist, large)


def _distance_bias(rel_bias):
    d = np.arange(WIN_MAX + 1)
    mult = np.zeros(WIN_MAX + 1, np.float64)
    for win, dil in DILATED:
        mult += (d <= win) & (d % dil == 0)
    member = mult > 0
    logm = jnp.asarray(np.log(np.where(member, mult, 1.0)), F32)
    bias = rel_bias[_t5_bucket(jnp.asarray(d, jnp.int32))].T.astype(F32)
    table = jnp.where(jnp.asarray(member)[None, :], bias + logm[None, :], NEG)
    return jnp.concatenate([table, jnp.full((SWA_HEADS, 1), NEG, F32)], axis=1)


def _toeplitz(table, dist):
    idx = np.where((dist >= 0) & (dist <= WIN_MAX), dist, WIN_MAX + 1)
    return table[:, idx]


def _ssm_tables(a_re, a_im, log_dt, b_re, b_im, c_re, c_im):
    step = jnp.exp(log_dt)[:, None]
    mag = jnp.exp(a_re * step)
    ab_re, ab_im = mag * jnp.cos(a_im * step), mag * jnp.sin(a_im * step)
    den = a_re * a_re + a_im * a_im
    cr = ((ab_re - 1.0) * a_re + ab_im * a_im) / den
    ci = (ab_im * a_re - (ab_re - 1.0) * a_im) / den
    bb_re = cr[..., None] * b_re - ci[..., None] * b_im
    bb_im = cr[..., None] * b_im + ci[..., None] * b_re
    pr, pi = [ab_re.reshape(1, SSM_W)], [ab_im.reshape(1, SSM_W)]
    for _ in range(7):
        pr, pi = (pr + [pr[-1] * pr[0] - pi[-1] * pi[0]], pi + [pr[-1] * pi[0] + pi[-1] * pr[0]])
    apr, api = jnp.concatenate(pr, axis=0), jnp.concatenate(pi, axis=0)
    bre = _block_diag(jnp.transpose(bb_re, (0, 2, 1))).astype(BF16)
    bim = _block_diag(jnp.transpose(bb_im, (0, 2, 1))).astype(BF16)
    cre = _block_diag(jnp.transpose(c_re, (0, 2, 1))).astype(BF16)
    cim = _block_diag(jnp.transpose(c_im, (0, 2, 1))).astype(BF16)
    return apr, api, bre, bim, cre, cim


def _permute_w_in(w_in):
    qkv, z, beta, dec, u, swa, lx, lg = jnp.split(w_in, np.cumsum([1536, 512, 4, 4, 512, 1536, 512])[:].tolist(), axis=-1)
    pad = jnp.zeros(w_in.shape[:-1] + (IN_W_PAD - IN_W,), w_in.dtype)
    return jnp.concatenate([qkv, z, u, swa, lx, lg, beta, dec, pad], axis=-1).astype(BF16)


def _trunk_layer(x, mem_k, mem_v, win, dn_state, dn_buf, ssm_re, ssm_im, lru_h, lru_buf, ffn_buf, lp, tiles):
    b, t, d = x.shape
    n = b * t
    proj = norm_matmul(x.reshape(n, d), lp['g_mix'], lp['w_in'], tiles['tm'], IN_TN).reshape(b, t, IN_W_PAD)
    o_a, dn_state_new, dn_buf_new = deltanet(proj, _pad_rows8(dn_buf), dn_state, lp['dn_conv_w'], lp['dn_alog_row'],
                                             lp['dn_dtb_row'], lp['dn_norm_g'], tiles['chunk'])
    o_b, hr, hi = s5_mixer(proj, ssm_re.reshape(b, 1, SSM_W), ssm_im.reshape(b, 1, SSM_W), *lp['ssm_tables'],
                           lp['ssm_d'], lp['ssm_w_glu'], lp['ssm_b_glu'], tiles['ts'])
    if win is None:
        o_c = swa_prompt(proj, lp['swa_bias_prompt'])
    else:
        o_c = swa_sample(proj, *win)
    o_d, lru_h_new, lru_buf_new = rglru(proj, _pad_rows8(lru_buf), lru_h.reshape(b, 1, GROUP_W), lp['lru_conv_w'],
                                        lp['lru_conv_b'], lp['lru_wa'], lp['lru_b_a'], lp['lru_wx'], lp['lru_b_x'],
                                        lp['lru_lam'], tiles['ts'])
    x2 = mix_out([o.reshape(n, GROUP_W) for o in (o_a, o_b, o_c, o_d)], lp['w_out'], x.reshape(n, d),
                 tiles['tm'], 1024).reshape(b, t, d)
    x3 = cross_attn(x2, lp['g_cross'], lp['w_mem_q'], mem_k, mem_v, lp['w_mem_o'], tiles['tmb'])
    x4, st_u, st_g = conv_ffn(x3, lp['g_ffn'], lp['w_up'], lp['ffn_conv_w'], lp['w_down'], _pad_rows8(ffn_buf),
                              tiles['tmb'], 512)
    k_new = proj[:, :, COL_SWA_K:COL_SWA_K + GROUP_W].reshape(b, t, SWA_HEADS, SWA_HD)
    v_new = proj[:, :, COL_SWA_V:COL_SWA_V + GROUP_W].reshape(b, t, SWA_HEADS, SWA_HD)
    ffn_buf_new = jnp.concatenate([st_u[:, -1, 6:8], st_g[:, -1, 6:8]], axis=-1)
    return (x4, k_new, v_new, dn_state_new, dn_buf_new, hr.reshape(b, SSM_GROUPS, SSM_STATE),
            hi.reshape(b, SSM_GROUPS, SSM_STATE), lru_h_new.reshape(b, GROUP_W), lru_buf_new, ffn_buf_new)


def kernel(x_prompt, x_sample, mem_prompt, cache_mem_k, cache_mem_v, cache_win_k, cache_win_v, state_delta, state_delta_conv, state_ssm_re, state_ssm_im, state_lru, state_lru_conv, state_ffn_conv, rel_bias, g_mix, w_in, dn_conv_w, dn_a_log, dn_dt_bias, dn_norm_g, ssm_a_re, ssm_a_im, ssm_log_dt, ssm_b_re, ssm_b_im, ssm_c_re, ssm_c_im, ssm_d, ssm_w_glu, ssm_b_glu, lru_conv_w, lru_conv_b, lru_w_a, lru_b_a, lru_w_x, lru_b_x, lru_lam, w_out, g_cross, g_mem, w_mem_q, w_mem_k, w_mem_v, w_mem_o, g_ffn, w_up, ffn_conv_w, w_down, g_final):
    bp, tp, d = x_prompt.shape
    bs, ts_, _ = x_sample.shape
    lb = cache_win_k.shape[2]
    assert tp % SWA_BLOCK == 0 and tp <= WIN_MAX and lb == WIN_MAX and PAST_LEN >= lb

    w_in_p = _permute_w_in(w_in)
    w_out_b, w_mem_q_b, w_mem_o_b = w_out.astype(BF16), w_mem_q.astype(BF16), w_mem_o.astype(BF16)
    w_mem_kv_b = jnp.concatenate([w_mem_k, w_mem_v], axis=-1).astype(BF16)
    w_up_b, w_down_b, w_glu_b = w_up.astype(BF16), w_down.astype(BF16), ssm_w_glu.astype(BF16)
    lane_pad = lambda v: jnp.pad(v, ((0, 0), (DN_HEADS, 128 - 2 * DN_HEADS)))[:, None, :]
    alog_rows, dtb_rows = lane_pad(dn_a_log), lane_pad(dn_dt_bias)
    table = _distance_bias(rel_bias)
    noff = min(tp, WIN_MAX) // SWA_BLOCK + 1
    rq = np.arange(SWA_BLOCK)
    dist_p = (np.arange(noff)[:, None, None] * SWA_BLOCK + rq[None, :, None] - rq[None, None, :])
    bias_prompt = _toeplitz(table, dist_p)
    tq = np.arange(ts_)
    bias_c = _toeplitz(table, tq[:, None] + lb - np.arange(lb)[None, :])
    bias_n = _toeplitz(table, tq[:, None] - tq[None, :])

    tiles_p = dict(tm=512, tmb=512, chunk=DN_CHUNK, ts=256)
    tiles_s = dict(tm=bs * ts_, tmb=ts_, chunk=ts_, ts=ts_)
    names = ('mem_k', 'mem_v', 'win_k', 'win_v', 'delta', 'delta_conv', 'ssm_re', 'ssm_im', 'lru', 'lru_conv', 'ffn_conv')
    P = {nm: [] for nm in names}
    S = {nm: [] for nm in names[2:]}
    yp, ys = x_prompt, x_sample
    hw = MEM_HEADS * MEM_HD
    for l in range(DEPTH):
        lp = {'g_mix': g_mix[l], 'w_in': w_in_p[l], 'dn_conv_w': dn_conv_w[l], 'dn_alog_row': alog_rows[l],
              'dn_dtb_row': dtb_rows[l], 'dn_norm_g': dn_norm_g[l],
              'ssm_tables': _ssm_tables(ssm_a_re[l], ssm_a_im[l], ssm_log_dt[l], ssm_b_re[l], ssm_b_im[l],
                                        ssm_c_re[l], ssm_c_im[l]),
              'ssm_d': ssm_d[l], 'ssm_w_glu': w_glu_b[l], 'ssm_b_glu': ssm_b_glu[l],
              'swa_bias_prompt': bias_prompt,
              'lru_conv_w': lru_conv_w[l], 'lru_conv_b': lru_conv_b[l],
              'lru_wa': _block_diag(lru_w_a[l]).astype(BF16), 'lru_b_a': lru_b_a[l],
              'lru_wx': _block_diag(lru_w_x[l]).astype(BF16), 'lru_b_x': lru_b_x[l], 'lru_lam': lru_lam[l],
              'w_out': w_out_b[l], 'g_cross': g_cross[l], 'w_mem_q': w_mem_q_b[l], 'w_mem_o': w_mem_o_b[l],
              'g_ffn': g_ffn[l], 'w_up': w_up_b[l], 'ffn_conv_w': ffn_conv_w[l], 'w_down': w_down_b[l]}
        mkv = norm_matmul(mem_prompt.reshape(bp * MEM_LEN, d), g_mem[l], w_mem_kv_b[l], 512, 2 * hw)
        mk = mkv[:, :hw].reshape(bp, MEM_LEN, hw)
        mv = mkv[:, hw:].reshape(bp, MEM_LEN, hw)
        zeros = lambda *shape: jnp.zeros((bp,) + shape, F32)
        yp, kp, vp, dsp, dcp, srp, sip, lhp, lcp, fcp = _trunk_layer(
            yp, mk, mv, None, zeros(DN_HEADS, DN_DK, DN_DK), zeros(3, 3 * GROUP_W), zeros(SSM_GROUPS, SSM_STATE),
            zeros(SSM_GROUPS, SSM_STATE), zeros(GROUP_W), zeros(3, GROUP_W), zeros(2, 2 * D_FF), lp, tiles_p)
        keep = min(WIN_MAX, tp)
        for nm, val in (('mem_k', mk.reshape(bp, MEM_LEN, MEM_HEADS, MEM_HD)),
                        ('mem_v', mv.reshape(bp, MEM_LEN, MEM_HEADS, MEM_HD)),
                        ('win_k', kp[:, tp - keep:]), ('win_v', vp[:, tp - keep:]), ('delta', dsp),
                        ('delta_conv', dcp), ('ssm_re', srp), ('ssm_im', sip), ('lru', lhp), ('lru_conv', lcp),
                        ('ffn_conv', fcp)):
            P[nm].append(val)
        win = (cache_win_k[l].reshape(bs, lb, GROUP_W), cache_win_v[l].reshape(bs, lb, GROUP_W), bias_c, bias_n)
        ys, ks_, vs_, dss, dcs, srs, sis, lhs, lcs, fcs = _trunk_layer(
            ys, cache_mem_k[l].reshape(bs, MEM_LEN, hw), cache_mem_v[l].reshape(bs, MEM_LEN, hw), win,
            state_delta[l], state_delta_conv[l], state_ssm_re[l], state_ssm_im[l], state_lru[l],
            state_lru_conv[l], state_ffn_conv[l], lp, tiles_s)
        for nm, val in (('win_k', ks_), ('win_v', vs_), ('delta', dss), ('delta_conv', dcs), ('ssm_re', srs),
                        ('ssm_im', sis), ('lru', lhs), ('lru_conv', lcs), ('ffn_conv', fcs)):
            S[nm].append(val)
    y_prompt = rmsnorm_rows(yp.reshape(bp * tp, d), g_final, 512).reshape(bp, tp, d)
    y_sample = rmsnorm_rows(ys.reshape(bs * ts_, d), g_final, bs * ts_).reshape(bs, ts_, d)
    return ((y_prompt, y_sample) + tuple(jnp.stack(P[nm]) for nm in names)
            + tuple(jnp.stack(S[nm]) for nm in names[2:]))
```

```python
import functools
import math

import numpy as np
import jax
import jax.numpy as jnp
from jax import lax
from jax.experimental import pallas as pl
from jax.experimental.pallas import tpu as pltpu

F32 = jnp.float32
BF16 = jnp.bfloat16
HIGHEST = lax.Precision.HIGHEST

D_MODEL = 2048
DEPTH = 4
PAST_LEN = 16384
GROUP_W = 512
DN_HEADS = 4
DN_DK = 128
DN_CHUNK = 64
SSM_CH = 16
SSM_GROUPS = 32
SSM_STATE = 64
SSM_W = SSM_GROUPS * SSM_STATE
SWA_HEADS = 8
SWA_HD = 64
DILATED = ((128, 1), (512, 4), (2048, 16))
WIN_MAX = 2048
SWA_BLOCK = 128
REL_BUCKETS = 32
REL_MAX_DIST = 2048
LRU_BLOCKS = 8
LRU_BW = 64
LRU_C = 8.0
MEM_LEN = 256
MEM_HEADS = 4
MEM_HD = 128
D_FF = 5632
EPS = 1e-6
NEG = -1e30

COL_DN_QKV = 0
COL_DN_Z = 1536
COL_SSM_U = 2048
COL_SWA_Q = 2560
COL_SWA_K = 3072
COL_SWA_V = 3584
COL_LRU_X = 4096
COL_LRU_G = 4608
COL_DN_BG = 5120
IN_W = 5128
IN_W_PAD = 5376
IN_TN = 768

VMEM_LIMIT = 56 * 1024 * 1024


def _cparams(n_axes):
    return pltpu.CompilerParams(dimension_semantics=("arbitrary",) * n_axes,
                                vmem_limit_bytes=VMEM_LIMIT)


def _rms(x, g):
    return x * lax.rsqrt(jnp.mean(x * x, axis=-1, keepdims=True) + EPS) * g


def _sigmoid(x):
    return jax.nn.sigmoid(x)


def _softplus(x):
    return jnp.maximum(x, 0.0) + jnp.log1p(jnp.exp(-jnp.abs(x)))


def _gelu(x):
    return 0.5 * x * (1.0 + jnp.tanh(math.sqrt(2.0 / math.pi) * (x + 0.044715 * (x * x * x))))


def _dot(a, b, prec=None):
    return jnp.dot(a, b, preferred_element_type=F32, precision=prec)


def _dot_nt(a, b, prec=None):
    return lax.dot_general(a, b, (((1,), (1,)), ((), ())), preferred_element_type=F32, precision=prec)


def _dot_tn(a, b, prec=None):
    return lax.dot_general(a, b, (((0,), (0,)), ((), ())), preferred_element_type=F32, precision=prec)


def _shift_rows(x, s, fill):
    rows = lax.broadcasted_iota(jnp.int32, x.shape, 0)
    return jnp.where(rows >= s, pltpu.roll(x, s, 0), fill)


def _nmm_kernel(x_ref, g_ref, w_ref, o_ref, xn_ref):
    @pl.when(pl.program_id(1) == 0)
    def _():
        xn_ref[...] = _rms(x_ref[...], g_ref[...]).astype(BF16)

    o_ref[...] = _dot(xn_ref[...], w_ref[...])


def norm_matmul(x, g, w, tm, tn):
    m, k = x.shape
    n = w.shape[1]
    return pl.pallas_call(
        _nmm_kernel,
        grid=(m // tm, n // tn),
        in_specs=[pl.BlockSpec((tm, k), lambda i, j: (i, 0)),
                  pl.BlockSpec((1, k), lambda i, j: (0, 0)),
                  pl.BlockSpec((k, tn), lambda i, j: (0, j))],
        out_specs=pl.BlockSpec((tm, tn), lambda i, j: (i, j)),
        out_shape=jax.ShapeDtypeStruct((m, n), F32),
        scratch_shapes=[pltpu.VMEM((tm, k), BF16)],
        compiler_params=_cparams(2),
        name="norm_matmul",
    )(x, g.reshape(1, k), w)


def _rmsnorm_kernel(x_ref, g_ref, o_ref):
    o_ref[...] = _rms(x_ref[...], g_ref[...])


def rmsnorm_rows(x, g, tm):
    m, k = x.shape
    return pl.pallas_call(
        _rmsnorm_kernel,
        grid=(m // tm,),
        in_specs=[pl.BlockSpec((tm, k), lambda i: (i, 0)), pl.BlockSpec((1, k), lambda i: (0, 0))],
        out_specs=pl.BlockSpec((tm, k), lambda i: (i, 0)),
        out_shape=jax.ShapeDtypeStruct((m, k), F32),
        compiler_params=_cparams(1),
        name="final_rmsnorm",
    )(x, g.reshape(1, k))


def _mix_out_kernel(a_ref, b_ref, c_ref, d_ref, w_ref, r_ref, o_ref):
    acc = r_ref[...]
    for i, part in enumerate((a_ref, b_ref, c_ref, d_ref)):
        acc = acc + _dot(part[...].astype(BF16), w_ref[i * GROUP_W:(i + 1) * GROUP_W, :])
    o_ref[...] = acc


def mix_out(parts, w, res, tm, tn):
    m, n = res.shape
    part_spec = pl.BlockSpec((tm, GROUP_W), lambda i, j: (i, 0))
    return pl.pallas_call(
        _mix_out_kernel,
        grid=(m // tm, n // tn),
        in_specs=[part_spec] * 4 + [pl.BlockSpec((4 * GROUP_W, tn), lambda i, j: (0, j)),
                                    pl.BlockSpec((tm, tn), lambda i, j: (i, j))],
        out_specs=pl.BlockSpec((tm, tn), lambda i, j: (i, j)),
        out_shape=jax.ShapeDtypeStruct((m, n), F32),
        compiler_params=_cparams(2),
        name="mix_out",
    )(*parts, w, res)


def _dn_kernel(qkv_ref, z_ref, bg_ref, buf_ref, s0_ref, cw_ref, alog_ref, dtb_ref, ng_ref,
               o_ref, s_ref, cbuf_ref, xx_ref, *, chunk):
    c = chunk
    n = pl.program_id(1)

    @pl.when(n == 0)
    def _():
        xx_ref[0:8, :] = buf_ref[0]
        s_ref[0] = s0_ref[0]

    xx_ref[8:8 + c, :] = qkv_ref[0]
    cw = cw_ref[...]
    y = (xx_ref[5:5 + c, :] * cw[0:1] + xx_ref[6:6 + c, :] * cw[1:2]
         + xx_ref[7:7 + c, :] * cw[2:3] + xx_ref[8:8 + c, :] * cw[3:4])
    y = y * _sigmoid(y)
    tail = xx_ref[c:c + 8, :]
    xx_ref[0:8, :] = tail
    cbuf_ref[0] = tail[5:8]

    bg = bg_ref[0]
    beta_all = _sigmoid(bg)
    g_all = -jnp.exp(alog_ref[...]) * _softplus(bg + dtb_ref[...])

    ri = lax.broadcasted_iota(jnp.int32, (c, c), 0)
    ci = lax.broadcasted_iota(jnp.int32, (c, c), 1)
    tril = ri >= ci
    strict = ri > ci
    m1 = tril.astype(F32)
    m2 = strict.astype(F32)
    eye = (ri == ci).astype(F32)
    gc_all = _dot(m1, g_all, HIGHEST)
    eg_all = jnp.exp(gc_all)
    el_all = jnp.exp(gc_all[c - 1:c, :] - gc_all)

    for h in range(DN_HEADS):
        lane = DN_HEADS + h
        q = y[:, h * DN_DK:(h + 1) * DN_DK]
        k = y[:, GROUP_W + h * DN_DK:GROUP_W + (h + 1) * DN_DK]
        v = y[:, 2 * GROUP_W + h * DN_DK:2 * GROUP_W + (h + 1) * DN_DK]
        q = q * lax.rsqrt(jnp.sum(q * q, axis=-1, keepdims=True) + EPS) * (DN_DK ** -0.5)
        k = k * lax.rsqrt(jnp.sum(k * k, axis=-1, keepdims=True) + EPS)
        beta = beta_all[:, h:h + 1]
        g_col = g_all[:, lane:lane + 1]
        eg = eg_all[:, lane:lane + 1]
        el = el_all[:, lane:lane + 1]
        eg_last = eg_all[c - 1:c, lane:lane + 1]
        dlog = _dot(m1, g_col * m2, HIGHEST)
        decay = jnp.where(tril, jnp.exp(jnp.where(tril, dlog, 0.0)), 0.0)
        kb = k * beta
        vb = v * beta
        mneg = -jnp.where(strict, _dot_nt(kb.astype(BF16), k.astype(BF16)) * decay, 0.0)
        tinv = eye + mneg
        p = mneg
        for _ in range(int(math.log2(c)) - 1):
            p = _dot(p, p, HIGHEST)
            tinv = tinv + _dot(tinv, p, HIGHEST)
        rhs = jnp.concatenate([vb, kb * eg], axis=-1)
        sol = _dot(tinv, rhs, HIGHEST)
        u = sol[:, :DN_DK]
        w = sol[:, DN_DK:]
        qk = _dot_nt(q.astype(BF16), k.astype(BF16)) * decay
        s = s_ref[0, h]
        sb = s.astype(BF16)
        v_new = u - _dot(w.astype(BF16), sb)
        o = _dot((q * eg).astype(BF16), sb) + _dot(qk.astype(BF16), v_new.astype(BF16))
        s_ref[0, h] = s * eg_last + _dot_tn((k * el).astype(BF16), v_new.astype(BF16))
        zh = z_ref[0, :, h * DN_DK:(h + 1) * DN_DK]
        o_ref[0, :, h * DN_DK:(h + 1) * DN_DK] = _rms(o, ng_ref[...]) * (zh * _sigmoid(zh))


def _bcast_lane(x, lane, width=128):
    return jnp.broadcast_to(x[:, lane:lane + 1], (x.shape[0], width))


def _heads512(x, base):
    return jnp.concatenate([_bcast_lane(x, base + h) for h in range(DN_HEADS)], axis=1)


def _heads256(x, base):
    first = lax.broadcasted_iota(jnp.int32, (x.shape[0], 128), 1) < DN_CHUNK
    return jnp.concatenate([jnp.where(first, _bcast_lane(x, base + 2 * i), _bcast_lane(x, base + 2 * i + 1))
                            for i in range(2)], axis=1)


def _split_bf16(a):
    hi = a.astype(BF16)
    return hi, (a - hi.astype(F32)).astype(BF16)


def _dot3(ah, al, bh, bl):
    return _dot(ah, bh) + _dot(al, bh) + _dot(ah, bl)


def _block_diag4(x):
    r, w = x.shape
    t = jnp.concatenate([x] * 4, axis=0)
    ri = lax.broadcasted_iota(jnp.int32, t.shape, 0) // r
    ci = lax.broadcasted_iota(jnp.int32, t.shape, 1) // (w // 4)
    return jnp.where(ri == ci, t, jnp.zeros_like(t))


def _dn_wide_kernel(qkv_ref, z_ref, bg_ref, buf_ref, s0_ref, cw_ref, alog_ref, dtb_ref, ng_ref,
                    o_ref, s_ref, cbuf_ref, xx_ref, y_ref, *, tb):
    c = DN_CHUNK
    nck = tb // c
    n = pl.program_id(1)

    @pl.when(n == 0)
    def _():
        xx_ref[0:8, :] = buf_ref[0]
        s_ref[0] = s0_ref[0]

    xx_ref[8:8 + tb, :] = qkv_ref[0]
    cw = cw_ref[...]
    y = (xx_ref[5:5 + tb, :] * cw[0:1] + xx_ref[6:6 + tb, :] * cw[1:2]
         + xx_ref[7:7 + tb, :] * cw[2:3] + xx_ref[8:8 + tb, :] * cw[3:4])
    y_ref[...] = y * _sigmoid(y)
    tail = xx_ref[tb:tb + 8, :]
    xx_ref[0:8, :] = tail
    cbuf_ref[0] = tail[5:8]
    for h in range(DN_HEADS):
        for base, scale in ((0, DN_DK ** -0.5), (GROUP_W, 1.0)):
            col = slice(base + h * DN_DK, base + (h + 1) * DN_DK)
            x = y_ref[:, col]
            y_ref[:, col] = x * (lax.rsqrt(jnp.sum(x * x, axis=-1, keepdims=True) + EPS) * scale)

    bg = bg_ref[0]
    beta_all = _sigmoid(bg)
    g_all = -jnp.exp(alog_ref[...]) * _softplus(bg + dtb_ref[...])
    ri = lax.broadcasted_iota(jnp.int32, (tb, tb), 0)
    ci = lax.broadcasted_iota(jnp.int32, (tb, tb), 1)
    same = (ri // c) == (ci // c)
    cum = jnp.where(same & (ri >= ci), 1.0, 0.0)
    gc_all = _dot(cum, g_all, HIGHEST)
    tot_all = _dot(jnp.where(same, 1.0, 0.0), g_all, HIGHEST)
    eg_all = jnp.exp(gc_all)
    el_all = jnp.exp(tot_all - gc_all)
    et_all = jnp.exp(tot_all)

    r64 = lax.broadcasted_iota(jnp.int32, (c, 2 * 128), 0)
    j64 = lax.broadcasted_iota(jnp.int32, (c, 2 * 128), 1) % c
    trilw = r64 >= j64
    strictw = r64 > j64
    eyew = jnp.where(r64 == j64, 1.0, 0.0)
    upw = jnp.where(r64 <= j64, 1.0, 0.0)
    ng = ng_ref[...]

    for ck in range(nck):
        rows = slice(ck * c, (ck + 1) * c)
        q = y_ref[rows, 0:GROUP_W]
        k = y_ref[rows, GROUP_W:2 * GROUP_W]
        v = y_ref[rows, 2 * GROUP_W:3 * GROUP_W]
        beta = _heads512(beta_all[rows], 0)
        eg = _heads512(eg_all[rows], DN_HEADS)
        el = _heads512(el_all[rows], DN_HEADS)
        g256 = _heads256(g_all[rows], DN_HEADS)
        gc256 = _heads256(gc_all[rows], DN_HEADS)
        grow = jnp.sum(g256 * upw, axis=0, keepdims=True)
        decay = jnp.where(trilw, jnp.exp(jnp.where(trilw, gc256 - grow, 0.0)), 0.0)
        kb = k * beta
        kbd = _block_diag4(k.astype(BF16))
        kq = _dot_nt(jnp.concatenate([kb.astype(BF16), q.astype(BF16)], axis=0), kbd)
        mneg = -jnp.where(strictw, kq[0:c] * decay, 0.0)
        qkd = kq[c:2 * c] * decay
        tw = eyew + mneg
        ph, plo = _split_bf16(mneg)
        p = _dot3(ph, plo, _block_diag4(ph), _block_diag4(plo))
        for it in range(5):
            ph, plo = _split_bf16(p)
            th, tlo = _split_bf16(tw)
            rh, rlo = _block_diag4(ph), _block_diag4(plo)
            if it < 4:
                both = _dot3(jnp.concatenate([ph, th], axis=0), jnp.concatenate([plo, tlo], axis=0), rh, rlo)
                p = both[0:c]
                tw = tw + both[c:2 * c]
            else:
                tw = tw + _dot3(th, tlo, rh, rlo)
        kbe = kb * eg
        vb = v * beta
        rhs = jnp.concatenate([jnp.concatenate([vb[:, h * DN_DK:(h + 1) * DN_DK], kbe[:, h * DN_DK:(h + 1) * DN_DK]],
                                               axis=1) for h in range(DN_HEADS)], axis=0)
        th, tlo = _split_bf16(tw)
        rh, rlo = _split_bf16(rhs)
        sol = _dot3(_block_diag4(th), _block_diag4(tlo), rh, rlo)
        qe = (q * eg).astype(BF16)
        ket = (k * el).T.astype(BF16)
        et = et_all[ck * c:ck * c + 1]
        vnew = []
        for h in range(DN_HEADS):
            hr = slice(h * c, (h + 1) * c)
            sb = s_ref[0, h].astype(BF16)
            vnew.append(sol[hr, 0:DN_DK] - _dot(sol[hr, DN_DK:2 * DN_DK].astype(BF16), sb))
        vnb = jnp.concatenate(vnew, axis=0).astype(BF16)
        intra = _dot(_block_diag4(qkd.astype(BF16)), vnb)
        for h in range(DN_HEADS):
            hr = slice(h * c, (h + 1) * c)
            hc = slice(h * DN_DK, (h + 1) * DN_DK)
            s = s_ref[0, h]
            o = _dot(qe[:, hc], s.astype(BF16)) + intra[hr]
            s_ref[0, h] = s * _bcast_lane(et, DN_HEADS + h) + _dot(ket[hc, :], vnb[hr])
            zh = z_ref[0, rows, hc]
            o_ref[0, rows, hc] = _rms(o, ng) * (zh * _sigmoid(zh))


DN_TILE = 256


def deltanet(proj, buf8, s0, conv_w, alog_row, dtb_row, norm_g, chunk):
    b, t, _ = proj.shape
    w3 = 3 * GROUP_W
    scratch = [pltpu.VMEM((8 + chunk, w3), F32)]
    if chunk == DN_CHUNK and t % DN_TILE == 0:
        chunk = DN_TILE
        kern = functools.partial(_dn_wide_kernel, tb=chunk)
        scratch = [pltpu.VMEM((8 + chunk, w3), F32), pltpu.VMEM((chunk, w3), F32)]
    else:
        kern = functools.partial(_dn_kernel, chunk=chunk)
    return pl.pallas_call(
        kern,
        grid=(b, t // chunk),
        in_specs=[pl.BlockSpec((1, chunk, w3), lambda bi, n: (bi, n, COL_DN_QKV // w3)),
                  pl.BlockSpec((1, chunk, GROUP_W), lambda bi, n: (bi, n, COL_DN_Z // GROUP_W)),
                  pl.BlockSpec((1, chunk, 128), lambda bi, n: (bi, n, COL_DN_BG // 128)),
                  pl.BlockSpec((1, 8, w3), lambda bi, n: (bi, 0, 0)),
                  pl.BlockSpec((1, DN_HEADS, DN_DK, DN_DK), lambda bi, n: (bi, 0, 0, 0)),
                  pl.BlockSpec((4, w3), lambda bi, n: (0, 0)),
                  pl.BlockSpec((1, 128), lambda bi, n: (0, 0)),
                  pl.BlockSpec((1, 128), lambda bi, n: (0, 0)),
                  pl.BlockSpec((1, DN_DK), lambda bi, n: (0, 0))],
        out_specs=[pl.BlockSpec((1, chunk, GROUP_W), lambda bi, n: (bi, n, 0)),
                   pl.BlockSpec((1, DN_HEADS, DN_DK, DN_DK), lambda bi, n: (bi, 0, 0, 0)),
                   pl.BlockSpec((1, 3, w3), lambda bi, n: (bi, 0, 0))],
        out_shape=[jax.ShapeDtypeStruct((b, t, GROUP_W), F32),
                   jax.ShapeDtypeStruct((b, DN_HEADS, DN_DK, DN_DK), F32),
                   jax.ShapeDtypeStruct((b, 3, w3), F32)],
        scratch_shapes=scratch,
        compiler_params=_cparams(2),
        name="deltanet",
    )(proj, proj, proj, buf8, s0, conv_w, alog_row, dtb_row, norm_g.reshape(1, DN_DK))


def _s5_kernel(u_ref, h0r_ref, h0i_ref, apr_ref, api_ref, bre_ref, bim_ref, cre_ref, cim_ref,
               d_ref, wg_ref, bgl_ref, y_ref, hr_ref, hi_ref, xr_ref, xi_ref, *, ts):
    @pl.when(pl.program_id(1) == 0)
    def _():
        hr_ref[0] = h0r_ref[0]
        hi_ref[0] = h0i_ref[0]

    u = u_ref[0]
    ub = u.astype(BF16)
    xr_ref[...] = _dot(ub, bre_ref[...])
    xi_ref[...] = _dot(ub, bim_ref[...])
    apr = apr_ref[...]
    api = api_ref[...]

    def group(gi, carry):
        cr, ci = carry
        r0 = pl.multiple_of(gi * 8, 8)
        xr = xr_ref[pl.ds(r0, 8), :]
        xi = xi_ref[pl.ds(r0, 8), :]
        for s in (1, 2, 4):
            ar = apr[s - 1:s]
            ai = api[s - 1:s]
            sr = _shift_rows(xr, s, 0.0)
            si = _shift_rows(xi, s, 0.0)
            xr, xi = xr + ar * sr - ai * si, xi + ar * si + ai * sr
        xr, xi = xr + apr * cr - api * ci, xi + apr * ci + api * cr
        xr_ref[pl.ds(r0, 8), :] = xr
        xi_ref[pl.ds(r0, 8), :] = xi
        return xr[7:8], xi[7:8]

    cr, ci = lax.fori_loop(0, ts // 8, group, (hr_ref[0], hi_ref[0]))
    hr_ref[0] = cr
    hi_ref[0] = ci
    y = _dot(xr_ref[...].astype(BF16), cre_ref[...]) - _dot(xi_ref[...].astype(BF16), cim_ref[...])
    y = _gelu(y + d_ref[...] * u)
    y_ref[0] = y * _sigmoid(_dot(y.astype(BF16), wg_ref[...]) + bgl_ref[...])


def s5_mixer(proj, h0r, h0i, apr, api, bre, bim, cre, cim, d_skip, w_glu, b_glu, ts):
    b, t, _ = proj.shape
    kern = functools.partial(_s5_kernel, ts=ts)
    full = lambda shape: pl.BlockSpec(shape, lambda bi, i: (0,) * len(shape))
    st_spec = pl.BlockSpec((1, 1, SSM_W), lambda bi, i: (bi, 0, 0))
    return pl.pallas_call(
        kern,
        grid=(b, t // ts),
        in_specs=[pl.BlockSpec((1, ts, GROUP_W), lambda bi, i: (bi, i, COL_SSM_U // GROUP_W)),
                  st_spec, st_spec,
                  full((8, SSM_W)), full((8, SSM_W)),
                  full((GROUP_W, SSM_W)), full((GROUP_W, SSM_W)),
                  full((SSM_W, GROUP_W)), full((SSM_W, GROUP_W)),
                  full((1, GROUP_W)), full((GROUP_W, GROUP_W)), full((1, GROUP_W))],
        out_specs=[pl.BlockSpec((1, ts, GROUP_W), lambda bi, i: (bi, i, 0)), st_spec, st_spec],
        out_shape=[jax.ShapeDtypeStruct((b, t, GROUP_W), F32),
                   jax.ShapeDtypeStruct((b, 1, SSM_W), F32),
                   jax.ShapeDtypeStruct((b, 1, SSM_W), F32)],
        scratch_shapes=[pltpu.VMEM((ts, SSM_W), F32), pltpu.VMEM((ts, SSM_W), F32)],
        compiler_params=_cparams(2),
        name="s5_mixer",
    )(proj, h0r, h0i, apr, api, bre, bim, cre, cim, d_skip.reshape(1, GROUP_W), w_glu,
      b_glu.reshape(1, GROUP_W))


def _swa_prompt_kernel(q_ref, k_ref, v_ref, bias_ref, o_ref, kt_ref, vb_ref, s_ref, *, nq, ch):
    qi = pl.program_id(2)
    bpc = ch // SWA_BLOCK
    nch = kt_ref.shape[0]

    @pl.when(qi == 0)
    def _():
        for c in range(nch):
            kt_ref[c] = k_ref[0, c * ch:(c + 1) * ch, :].T.astype(BF16)
            vb_ref[c] = v_ref[0, c * ch:(c + 1) * ch, :].astype(BF16)

    q2 = q_ref[0] * (SWA_HD ** -0.5)
    lane = lax.broadcasted_iota(jnp.int32, q2.shape, 1)
    first = lane < SWA_HD
    qm = (jnp.where(first, q2, 0.0).astype(BF16), jnp.where(first, 0.0, q2).astype(BF16))
    nchunks = qi // bpc + 1

    def pass1(c, mx):
        new = []
        for hh in range(2):
            s = _dot(qm[hh], kt_ref[c])
            m = mx[hh]
            for jj in range(bpc):
                sj = s[:, jj * SWA_BLOCK:(jj + 1) * SWA_BLOCK] + bias_ref[hh, (nq - 1) - qi + c * bpc + jj]
                s_ref[hh, c, :, jj * SWA_BLOCK:(jj + 1) * SWA_BLOCK] = sj
                m = jnp.maximum(m, sj)
            new.append(m)
        return tuple(new)

    neg = jnp.full((SWA_BLOCK, SWA_BLOCK), NEG, F32)
    mx = lax.fori_loop(0, nchunks, pass1, (neg, neg))
    mrow = [jnp.max(m, axis=-1, keepdims=True) for m in mx]

    def pass2(c, carry):
        new = []
        for hh in range(2):
            ls, acc = carry[hh]
            p = jnp.exp(s_ref[hh, c] - mrow[hh])
            for jj in range(bpc):
                ls = ls + p[:, jj * SWA_BLOCK:(jj + 1) * SWA_BLOCK]
            acc = acc + _dot(p.astype(BF16), vb_ref[c])
            new.append((ls, acc))
        return tuple(new)

    zero = jnp.zeros((SWA_BLOCK, SWA_BLOCK), F32)
    (ls0, acc0), (ls1, acc1) = lax.fori_loop(0, nchunks, pass2, ((zero, zero), (zero, zero)))
    out0 = acc0 / jnp.sum(ls0, axis=-1, keepdims=True)
    out1 = acc1 / jnp.sum(ls1, axis=-1, keepdims=True)
    o_ref[0] = jnp.where(first, out0, out1)


def _swa_chunk(t):
    return 512 if t % 512 == 0 else SWA_BLOCK


def swa_prompt(proj, bias):
    b, t, _ = proj.shape
    nq = t // SWA_BLOCK
    ch = _swa_chunk(t)
    nch = t // ch
    nblk = bias.shape[1]
    assert nblk == nq + ch // SWA_BLOCK - 1
    qcol, kcol, vcol = COL_SWA_Q // 128, COL_SWA_K // 128, COL_SWA_V // 128
    kern = functools.partial(_swa_prompt_kernel, nq=nq, ch=ch)
    return pl.pallas_call(
        kern,
        grid=(b, SWA_HEADS // 2, nq),
        in_specs=[pl.BlockSpec((1, SWA_BLOCK, 128), lambda bi, hp, qi: (bi, qi, qcol + hp)),
                  pl.BlockSpec((1, t, 128), lambda bi, hp, qi: (bi, 0, kcol + hp)),
                  pl.BlockSpec((1, t, 128), lambda bi, hp, qi: (bi, 0, vcol + hp)),
                  pl.BlockSpec((2, nblk, SWA_BLOCK, SWA_BLOCK), lambda bi, hp, qi: (hp, 0, 0, 0))],
        out_specs=pl.BlockSpec((1, SWA_BLOCK, 128), lambda bi, hp, qi: (bi, qi, hp)),
        out_shape=jax.ShapeDtypeStruct((b, t, GROUP_W), F32),
        scratch_shapes=[pltpu.VMEM((nch, 128, ch), BF16), pltpu.VMEM((nch, ch, 128), BF16),
                        pltpu.VMEM((2, nch, SWA_BLOCK, ch), F32)],
        compiler_params=_cparams(3),
        name="swa_prompt",
    )(proj, proj, proj, bias)


def _swa_sample_kernel(q_ref, kn_ref, vn_ref, kc_ref, vc_ref, bc_ref, bn_ref, o_ref):
    q = q_ref[0] * (SWA_HD ** -0.5)
    kn = kn_ref[0]
    vn = vn_ref[0]
    for h in range(SWA_HEADS):
        lo, hi = h * SWA_HD, (h + 1) * SWA_HD
        qh = q[:, lo:hi].astype(BF16)
        sc = _dot_nt(qh, kc_ref[0, :, lo:hi].astype(BF16)) + bc_ref[h]
        sn = _dot_nt(qh, kn[:, lo:hi].astype(BF16)) + bn_ref[h]
        m = jnp.maximum(jnp.max(sc, axis=-1, keepdims=True), jnp.max(sn, axis=-1, keepdims=True))
        pc = jnp.exp(sc - m)
        pn = jnp.exp(sn - m)
        l = jnp.sum(pc, axis=-1, keepdims=True) + jnp.sum(pn, axis=-1, keepdims=True)
        acc = (_dot(pc.astype(BF16), vc_ref[0, :, lo:hi].astype(BF16))
               + _dot(pn.astype(BF16), vn[:, lo:hi].astype(BF16)))
        o_ref[0, :, lo:hi] = acc / l


def swa_sample(proj, cache_k, cache_v, bias_c, bias_n):
    b, t, _ = proj.shape
    lb = cache_k.shape[1]
    new_spec = lambda col: pl.BlockSpec((1, t, GROUP_W), lambda bi: (bi, 0, col // GROUP_W))
    cache_spec = pl.BlockSpec((1, lb, GROUP_W), lambda bi: (bi, 0, 0))
    return pl.pallas_call(
        _swa_sample_kernel,
        grid=(b,),
        in_specs=[new_spec(COL_SWA_Q), new_spec(COL_SWA_K), new_spec(COL_SWA_V), cache_spec, cache_spec,
                  pl.BlockSpec((SWA_HEADS, t, lb), lambda bi: (0, 0, 0)),
                  pl.BlockSpec((SWA_HEADS, t, t), lambda bi: (0, 0, 0))],
        out_specs=pl.BlockSpec((1, t, GROUP_W), lambda bi: (bi, 0, 0)),
        out_shape=jax.ShapeDtypeStruct((b, t, GROUP_W), F32),
        compiler_params=_cparams(1),
        name="swa_sample",
    )(proj, proj, proj, cache_k, cache_v, bias_c, bias_n)


def _lru_kernel(x_ref, gate_ref, buf_ref, h0_ref, cw_ref, cb_ref, wa_ref, ba_ref, wx_ref, bx_ref, lam_ref,
                y_ref, h_ref, cbuf_ref, xx_ref, a_ref, b_ref, *, ts):
    @pl.when(pl.program_id(1) == 0)
    def _():
        xx_ref[0:8, :] = buf_ref[0]
        h_ref[0] = h0_ref[0]

    xx_ref[8:8 + ts, :] = x_ref[0]
    cw = cw_ref[...]
    xc = (xx_ref[5:5 + ts, :] * cw[0:1] + xx_ref[6:6 + ts, :] * cw[1:2]
          + xx_ref[7:7 + ts, :] * cw[2:3] + xx_ref[8:8 + ts, :] * cw[3:4]) + cb_ref[...]
    tail = xx_ref[ts:ts + 8, :]
    xx_ref[0:8, :] = tail
    cbuf_ref[0] = tail[5:8]

    xcb = xc.astype(BF16)
    r = _sigmoid(_dot(xcb, wa_ref[...]) + ba_ref[...])
    ig = _sigmoid(_dot(xcb, wx_ref[...]) + bx_ref[...])
    log_a = -LRU_C * r * _softplus(-lam_ref[...])
    a = jnp.exp(log_a)
    mult = jnp.sqrt(jnp.maximum(-jnp.tanh(log_a) * (a * a + 1.0), 0.0))
    a_ref[...] = a
    b_ref[...] = mult * ig * xc

    def group(gi, carry):
        r0 = pl.multiple_of(gi * 8, 8)
        av = a_ref[pl.ds(r0, 8), :]
        bv = b_ref[pl.ds(r0, 8), :]
        for s in (1, 2, 4):
            a_s = _shift_rows(av, s, 1.0)
            b_s = _shift_rows(bv, s, 0.0)
            bv = av * b_s + bv
            av = av * a_s
        hv = bv + av * carry
        b_ref[pl.ds(r0, 8), :] = hv
        return hv[7:8]

    h_ref[0] = lax.fori_loop(0, ts // 8, group, h_ref[0])
    y_ref[0] = b_ref[...] * _gelu(gate_ref[0])


def rglru(proj, buf8, h0, conv_w, conv_b, wa, ba, wx, bx, lam, ts):
    b, t, _ = proj.shape
    kern = functools.partial(_lru_kernel, ts=ts)
    full = lambda shape: pl.BlockSpec(shape, lambda bi, i: (0,) * len(shape))
    row = lambda v: v.reshape(1, GROUP_W)
    return pl.pallas_call(
        kern,
        grid=(b, t // ts),
        in_specs=[pl.BlockSpec((1, ts, GROUP_W), lambda bi, i: (bi, i, COL_LRU_X // GROUP_W)),
                  pl.BlockSpec((1, ts, GROUP_W), lambda bi, i: (bi, i, COL_LRU_G // GROUP_W)),
                  pl.BlockSpec((1, 8, GROUP_W), lambda bi, i: (bi, 0, 0)),
                  pl.BlockSpec((1, 1, GROUP_W), lambda bi, i: (bi, 0, 0)),
                  full((4, GROUP_W)), full((1, GROUP_W)),
                  full((GROUP_W, GROUP_W)), full((1, GROUP_W)),
                  full((GROUP_W, GROUP_W)), full((1, GROUP_W)), full((1, GROUP_W))],
        out_specs=[pl.BlockSpec((1, ts, GROUP_W), lambda bi, i: (bi, i, 0)),
                   pl.BlockSpec((1, 1, GROUP_W), lambda bi, i: (bi, 0, 0)),
                   pl.BlockSpec((1, 3, GROUP_W), lambda bi, i: (bi, 0, 0))],
        out_shape=[jax.ShapeDtypeStruct((b, t, GROUP_W), F32),
                   jax.ShapeDtypeStruct((b, 1, GROUP_W), F32),
                   jax.ShapeDtypeStruct((b, 3, GROUP_W), F32)],
        scratch_shapes=[pltpu.VMEM((8 + ts, GROUP_W), F32), pltpu.VMEM((ts, GROUP_W), F32),
                        pltpu.VMEM((ts, GROUP_W), F32)],
        compiler_params=_cparams(2),
        name="rglru",
    )(proj, proj, buf8, h0, conv_w, row(conv_b), wa, row(ba), wx, row(bx), row(lam))


def _cross_kernel(x_ref, g_ref, wq_ref, k_ref, v_ref, wo_ref, o_ref):
    x = x_ref[0]
    xn = _rms(x, g_ref[...]).astype(BF16)
    q = _dot(xn, wq_ref[...]) * (MEM_HD ** -0.5)
    outs = []
    for h in range(MEM_HEADS):
        lo, hi = h * MEM_HD, (h + 1) * MEM_HD
        s = _dot_nt(q[:, lo:hi].astype(BF16), k_ref[0, :, lo:hi].astype(BF16))
        m = jnp.max(s, axis=-1, keepdims=True)
        p = jnp.exp(s - m)
        l = jnp.sum(p, axis=-1, keepdims=True)
        outs.append(_dot((p / l).astype(BF16), v_ref[0, :, lo:hi].astype(BF16)))
    o = jnp.concatenate(outs, axis=-1)
    o_ref[0] = x + _dot(o.astype(BF16), wo_ref[...])


def cross_attn(x, g, wq, mem_k, mem_v, wo, tm):
    b, t, d = x.shape
    hw = MEM_HEADS * MEM_HD
    full = lambda shape: pl.BlockSpec(shape, lambda bi, i: (0,) * len(shape))
    return pl.pallas_call(
        _cross_kernel,
        grid=(b, t // tm),
        in_specs=[pl.BlockSpec((1, tm, d), lambda bi, i: (bi, i, 0)),
                  full((1, d)), full((d, hw)),
                  pl.BlockSpec((1, MEM_LEN, hw), lambda bi, i: (bi, 0, 0)),
                  pl.BlockSpec((1, MEM_LEN, hw), lambda bi, i: (bi, 0, 0)),
                  full((hw, d))],
        out_specs=pl.BlockSpec((1, tm, d), lambda bi, i: (bi, i, 0)),
        out_shape=jax.ShapeDtypeStruct((b, t, d), F32),
        compiler_params=_cparams(2),
        name="cross_attn",
    )(x, g.reshape(1, d), wq, mem_k, mem_v, wo)


def _ffn_kernel(x_ref, g_ref, wu_ref, wg_ref, cwu_ref, cwg_ref, wd_ref, bufu_ref, bufg_ref,
                o_ref, stu_ref, stg_ref, xn_ref, acc_ref, halo_u, halo_g, work_u, work_g, *, tm, nj):
    i = pl.program_id(1)
    j = pl.program_id(2)

    @pl.when(j == 0)
    def _():
        xn_ref[...] = _rms(x_ref[0], g_ref[...]).astype(BF16)

    @pl.when(i == 0)
    def _():
        halo_u[j] = bufu_ref[0]
        halo_g[j] = bufg_ref[0]

    def conv_half(w_ref, cw_ref, halo, work, st_ref):
        hup = _dot(xn_ref[...], w_ref[...])
        work[0:8, :] = halo[j]
        work[8:8 + tm, :] = hup
        cw = cw_ref[...]
        hc = work[6:6 + tm, :] * cw[0:1] + work[7:7 + tm, :] * cw[1:2] + hup * cw[2:3]
        tail = work[tm:tm + 8, :]
        halo[j] = tail
        st_ref[0, 0] = tail
        return hc

    cu = conv_half(wu_ref, cwu_ref, halo_u, work_u, stu_ref)
    cg = conv_half(wg_ref, cwg_ref, halo_g, work_g, stg_ref)
    act = (cg * _sigmoid(cg)) * cu
    part = _dot(act.astype(BF16), wd_ref[...])

    @pl.when(j == 0)
    def _():
        acc_ref[...] = part

    @pl.when(j > 0)
    def _():
        acc_ref[...] += part

    @pl.when(j == nj - 1)
    def _():
        o_ref[0] = x_ref[0] + acc_ref[...]


def conv_ffn(x, g, w_up, conv_w, w_down, buf8, tm, tf):
    b, t, d = x.shape
    nj = D_FF // tf
    kern = functools.partial(_ffn_kernel, tm=tm, nj=nj)
    return pl.pallas_call(
        kern,
        grid=(b, t // tm, nj),
        in_specs=[pl.BlockSpec((1, tm, d), lambda bi, i, j: (bi, i, 0)),
                  pl.BlockSpec((1, d), lambda bi, i, j: (0, 0)),
                  pl.BlockSpec((d, tf), lambda bi, i, j: (0, j)),
                  pl.BlockSpec((d, tf), lambda bi, i, j: (0, j + nj)),
                  pl.BlockSpec((3, tf), lambda bi, i, j: (0, j)),
                  pl.BlockSpec((3, tf), lambda bi, i, j: (0, j + nj)),
                  pl.BlockSpec((tf, d), lambda bi, i, j: (j, 0)),
                  pl.BlockSpec((1, 8, tf), lambda bi, i, j: (bi, 0, j)),
                  pl.BlockSpec((1, 8, tf), lambda bi, i, j: (bi, 0, j + nj))],
        out_specs=[pl.BlockSpec((1, tm, d), lambda bi, i, j: (bi, i, 0)),
                   pl.BlockSpec((1, 1, 8, tf), lambda bi, i, j: (bi, i, 0, j)),
                   pl.BlockSpec((1, 1, 8, tf), lambda bi, i, j: (bi, i, 0, j))],
        out_shape=[jax.ShapeDtypeStruct((b, t, d), F32),
                   jax.ShapeDtypeStruct((b, t // tm, 8, D_FF), F32),
                   jax.ShapeDtypeStruct((b, t // tm, 8, D_FF), F32)],
        scratch_shapes=[pltpu.VMEM((tm, d), BF16), pltpu.VMEM((tm, d), F32),
                        pltpu.VMEM((nj, 8, tf), F32), pltpu.VMEM((nj, 8, tf), F32),
                        pltpu.VMEM((8 + tm, tf), F32), pltpu.VMEM((8 + tm, tf), F32)],
        compiler_params=_cparams(3),
        name="conv_ffn",
    )(x, g.reshape(1, d), w_up, w_up, conv_w, conv_w, w_down, buf8, buf8)


def _pad_rows8(buf):
    return jnp.pad(buf, ((0, 0), (8 - buf.shape[1], 0), (0, 0)))


def _block_diag(blocks):
    g, r, c = blocks.shape
    eye = jnp.eye(g, dtype=blocks.dtype)
    return (eye[:, None, :, None] * blocks[:, :, None, :]).reshape(g * r, g * c)


def _t5_bucket(dist):
    exact = REL_BUCKETS // 2
    d = jnp.maximum(dist.astype(F32), 1.0)
    large = exact + (jnp.log(d / exact) / math.log(REL_MAX_DIST / exact) * (REL_BUCKETS - exact)).astype(jnp.int32)
    large = jnp.minimum(large, REL_BUCKETS - 1)
    return jnp.where(dist < exact, dist, large)


def _distance_bias(rel_bias):
    d = np.arange(WIN_MAX + 1)
    mult = np.zeros(WIN_MAX + 1, np.float64)
    for win, dil in DILATED:
        mult += (d <= win) & (d % dil == 0)
    member = mult > 0
    logm = jnp.asarray(np.log(np.where(member, mult, 1.0)), F32)
    bias = rel_bias[_t5_bucket(jnp.asarray(d, jnp.int32))].T.astype(F32)
    table = jnp.where(jnp.asarray(member)[None, :], bias + logm[None, :], NEG)
    return jnp.concatenate([table, jnp.full((SWA_HEADS, 1), NEG, F32)], axis=1)


def _prompt_bias_blocks(table, t):
    nq = t // SWA_BLOCK
    nblk = nq + _swa_chunk(t) // SWA_BLOCK - 1
    w = nblk * SWA_BLOCK
    p = w + SWA_BLOCK
    h = table.shape[0]
    seq = jnp.concatenate([table[:, :t + 1][:, ::-1], jnp.full((h, p - (t + 1)), NEG, F32)], axis=1)
    r_ext = jnp.concatenate([seq[:, SWA_BLOCK:], seq[:, :SWA_BLOCK]], axis=1)
    strip = jnp.tile(r_ext, (1, SWA_BLOCK))[:, :SWA_BLOCK * (p - 1)].reshape(h, SWA_BLOCK, p - 1)[:, :, :w]
    return strip.reshape(h, SWA_BLOCK, nblk, SWA_BLOCK).transpose(0, 2, 1, 3)


def _toeplitz(table, dist):
    idx = np.where((dist >= 0) & (dist <= WIN_MAX), dist, WIN_MAX + 1)
    return table[:, idx]


def _ssm_tables(a_re, a_im, log_dt, b_re, b_im, c_re, c_im):
    step = jnp.exp(log_dt)[:, None]
    mag = jnp.exp(a_re * step)
    ab_re, ab_im = mag * jnp.cos(a_im * step), mag * jnp.sin(a_im * step)
    den = a_re * a_re + a_im * a_im
    cr = ((ab_re - 1.0) * a_re + ab_im * a_im) / den
    ci = (ab_im * a_re - (ab_re - 1.0) * a_im) / den
    bb_re = cr[..., None] * b_re - ci[..., None] * b_im
    bb_im = cr[..., None] * b_im + ci[..., None] * b_re
    pr, pi = [ab_re.reshape(1, SSM_W)], [ab_im.reshape(1, SSM_W)]
    for _ in range(7):
        pr, pi = (pr + [pr[-1] * pr[0] - pi[-1] * pi[0]], pi + [pr[-1] * pi[0] + pi[-1] * pr[0]])
    apr, api = jnp.concatenate(pr, axis=0), jnp.concatenate(pi, axis=0)
    bre = _block_diag(jnp.transpose(bb_re, (0, 2, 1))).astype(BF16)
    bim = _block_diag(jnp.transpose(bb_im, (0, 2, 1))).astype(BF16)
    cre = _block_diag(jnp.transpose(c_re, (0, 2, 1))).astype(BF16)
    cim = _block_diag(jnp.transpose(c_im, (0, 2, 1))).astype(BF16)
    return apr, api, bre, bim, cre, cim


def _permute_w_in(w_in):
    qkv, z, beta, dec, u, swa, lx, lg = jnp.split(w_in, np.cumsum([1536, 512, 4, 4, 512, 1536, 512])[:].tolist(), axis=-1)
    pad = jnp.zeros(w_in.shape[:-1] + (IN_W_PAD - IN_W,), w_in.dtype)
    return jnp.concatenate([qkv, z, u, swa, lx, lg, beta, dec, pad], axis=-1).astype(BF16)


def _trunk_layer(x, mem_k, mem_v, win, dn_state, dn_buf, ssm_re, ssm_im, lru_h, lru_buf, ffn_buf, lp, tiles):
    b, t, d = x.shape
    n = b * t
    proj = norm_matmul(x.reshape(n, d), lp['g_mix'], lp['w_in'], tiles['tm'], IN_TN).reshape(b, t, IN_W_PAD)
    o_a, dn_state_new, dn_buf_new = deltanet(proj, _pad_rows8(dn_buf), dn_state, lp['dn_conv_w'], lp['dn_alog_row'],
                                             lp['dn_dtb_row'], lp['dn_norm_g'], tiles['chunk'])
    o_b, hr, hi = s5_mixer(proj, ssm_re.reshape(b, 1, SSM_W), ssm_im.reshape(b, 1, SSM_W), *lp['ssm_tables'],
                           lp['ssm_d'], lp['ssm_w_glu'], lp['ssm_b_glu'], tiles['ts'])
    if win is None:
        o_c = swa_prompt(proj, lp['swa_bias_prompt'])
    else:
        o_c = swa_sample(proj, *win)
    o_d, lru_h_new, lru_buf_new = rglru(proj, _pad_rows8(lru_buf), lru_h.reshape(b, 1, GROUP_W), lp['lru_conv_w'],
                                        lp['lru_conv_b'], lp['lru_wa'], lp['lru_b_a'], lp['lru_wx'], lp['lru_b_x'],
                                        lp['lru_lam'], tiles['ts'])
    x2 = mix_out([o.reshape(n, GROUP_W) for o in (o_a, o_b, o_c, o_d)], lp['w_out'], x.reshape(n, d),
                 tiles['tm'], 1024).reshape(b, t, d)
    x3 = cross_attn(x2, lp['g_cross'], lp['w_mem_q'], mem_k, mem_v, lp['w_mem_o'], tiles['tmb'])
    x4, st_u, st_g = conv_ffn(x3, lp['g_ffn'], lp['w_up'], lp['ffn_conv_w'], lp['w_down'], _pad_rows8(ffn_buf),
                              tiles['tmb'], 512)
    k_new = proj[:, :, COL_SWA_K:COL_SWA_K + GROUP_W].reshape(b, t, SWA_HEADS, SWA_HD)
    v_new = proj[:, :, COL_SWA_V:COL_SWA_V + GROUP_W].reshape(b, t, SWA_HEADS, SWA_HD)
    ffn_buf_new = jnp.concatenate([st_u[:, -1, 6:8], st_g[:, -1, 6:8]], axis=-1)
    return (x4, k_new, v_new, dn_state_new, dn_buf_new, hr.reshape(b, SSM_GROUPS, SSM_STATE),
            hi.reshape(b, SSM_GROUPS, SSM_STATE), lru_h_new.reshape(b, GROUP_W), lru_buf_new, ffn_buf_new)


def kernel(x_prompt, x_sample, mem_prompt, cache_mem_k, cache_mem_v, cache_win_k, cache_win_v, state_delta, state_delta_conv, state_ssm_re, state_ssm_im, state_lru, state_lru_conv, state_ffn_conv, rel_bias, g_mix, w_in, dn_conv_w, dn_a_log, dn_dt_bias, dn_norm_g, ssm_a_re, ssm_a_im, ssm_log_dt, ssm_b_re, ssm_b_im, ssm_c_re, ssm_c_im, ssm_d, ssm_w_glu, ssm_b_glu, lru_conv_w, lru_conv_b, lru_w_a, lru_b_a, lru_w_x, lru_b_x, lru_lam, w_out, g_cross, g_mem, w_mem_q, w_mem_k, w_mem_v, w_mem_o, g_ffn, w_up, ffn_conv_w, w_down, g_final):
    bp, tp, d = x_prompt.shape
    bs, ts_, _ = x_sample.shape
    lb = cache_win_k.shape[2]
    assert tp % SWA_BLOCK == 0 and tp <= WIN_MAX and lb == WIN_MAX and PAST_LEN >= lb

    w_in_p = _permute_w_in(w_in)
    w_out_b, w_mem_q_b, w_mem_o_b = w_out.astype(BF16), w_mem_q.astype(BF16), w_mem_o.astype(BF16)
    w_mem_kv_b = jnp.concatenate([w_mem_k, w_mem_v], axis=-1).astype(BF16)
    w_up_b, w_down_b, w_glu_b = w_up.astype(BF16), w_down.astype(BF16), ssm_w_glu.astype(BF16)
    lane_pad = lambda v: jnp.pad(v, ((0, 0), (DN_HEADS, 128 - 2 * DN_HEADS)))[:, None, :]
    alog_rows, dtb_rows = lane_pad(dn_a_log), lane_pad(dn_dt_bias)
    table = _distance_bias(rel_bias)
    bias_prompt = _prompt_bias_blocks(table, tp)
    tq = np.arange(ts_)
    bias_c = _toeplitz(table, tq[:, None] + lb - np.arange(lb)[None, :])
    bias_n = _toeplitz(table, tq[:, None] - tq[None, :])

    tiles_p = dict(tm=512, tmb=512, chunk=DN_CHUNK, ts=256)
    tiles_s = dict(tm=bs * ts_, tmb=ts_, chunk=ts_, ts=ts_)
    names = ('mem_k', 'mem_v', 'win_k', 'win_v', 'delta', 'delta_conv', 'ssm_re', 'ssm_im', 'lru', 'lru_conv', 'ffn_conv')
    P = {nm: [] for nm in names}
    S = {nm: [] for nm in names[2:]}
    yp, ys = x_prompt, x_sample
    hw = MEM_HEADS * MEM_HD
    for l in range(DEPTH):
        lp = {'g_mix': g_mix[l], 'w_in': w_in_p[l], 'dn_conv_w': dn_conv_w[l], 'dn_alog_row': alog_rows[l],
              'dn_dtb_row': dtb_rows[l], 'dn_norm_g': dn_norm_g[l],
              'ssm_tables': _ssm_tables(ssm_a_re[l], ssm_a_im[l], ssm_log_dt[l], ssm_b_re[l], ssm_b_im[l],
                                        ssm_c_re[l], ssm_c_im[l]),
              'ssm_d': ssm_d[l], 'ssm_w_glu': w_glu_b[l], 'ssm_b_glu': ssm_b_glu[l],
              'swa_bias_prompt': bias_prompt,
              'lru_conv_w': lru_conv_w[l], 'lru_conv_b': lru_conv_b[l],
              'lru_wa': _block_diag(lru_w_a[l]).astype(BF16), 'lru_b_a': lru_b_a[l],
              'lru_wx': _block_diag(lru_w_x[l]).astype(BF16), 'lru_b_x': lru_b_x[l], 'lru_lam': lru_lam[l],
              'w_out': w_out_b[l], 'g_cross': g_cross[l], 'w_mem_q': w_mem_q_b[l], 'w_mem_o': w_mem_o_b[l],
              'g_ffn': g_ffn[l], 'w_up': w_up_b[l], 'ffn_conv_w': ffn_conv_w[l], 'w_down': w_down_b[l]}
        mkv = norm_matmul(mem_prompt.reshape(bp * MEM_LEN, d), g_mem[l], w_mem_kv_b[l], 512, 2 * hw)
        mk = mkv[:, :hw].reshape(bp, MEM_LEN, hw)
        mv = mkv[:, hw:].reshape(bp, MEM_LEN, hw)
        zeros = lambda *shape: jnp.zeros((bp,) + shape, F32)
        yp, kp, vp, dsp, dcp, srp, sip, lhp, lcp, fcp = _trunk_layer(
            yp, mk, mv, None, zeros(DN_HEADS, DN_DK, DN_DK), zeros(3, 3 * GROUP_W), zeros(SSM_GROUPS, SSM_STATE),
            zeros(SSM_GROUPS, SSM_STATE), zeros(GROUP_W), zeros(3, GROUP_W), zeros(2, 2 * D_FF), lp, tiles_p)
        keep = min(WIN_MAX, tp)
        for nm, val in (('mem_k', mk.reshape(bp, MEM_LEN, MEM_HEADS, MEM_HD)),
                        ('mem_v', mv.reshape(bp, MEM_LEN, MEM_HEADS, MEM_HD)),
                        ('win_k', kp[:, tp - keep:]), ('win_v', vp[:, tp - keep:]), ('delta', dsp),
                        ('delta_conv', dcp), ('ssm_re', srp), ('ssm_im', sip), ('lru', lhp), ('lru_conv', lcp),
                        ('ffn_conv', fcp)):
            P[nm].append(val)
        win = (cache_win_k[l].reshape(bs, lb, GROUP_W), cache_win_v[l].reshape(bs, lb, GROUP_W), bias_c, bias_n)
        ys, ks_, vs_, dss, dcs, srs, sis, lhs, lcs, fcs = _trunk_layer(
            ys, cache_mem_k[l].reshape(bs, MEM_LEN, hw), cache_mem_v[l].reshape(bs, MEM_LEN, hw), win,
            state_delta[l], state_delta_conv[l], state_ssm_re[l], state_ssm_im[l], state_lru[l],
            state_lru_conv[l], state_ffn_conv[l], lp, tiles_s)
        for nm, val in (('win_k', ks_), ('win_v', vs_), ('delta', dss), ('delta_conv', dcs), ('ssm_re', srs),
                        ('ssm_im', sis), ('lru', lhs), ('lru_conv', lcs), ('ffn_conv', fcs)):
            S[nm].append(val)
    y_prompt = rmsnorm_rows(yp.reshape(bp * tp, d), g_final, 512).reshape(bp, tp, d)
    y_sample = rmsnorm_rows(ys.reshape(bs * ts_, d), g_final, bs * ts_).reshape(bs, ts_, d)
    return ((y_prompt, y_sample) + tuple(jnp.stack(P[nm]) for nm in names)
            + tuple(jnp.stack(S[nm]) for nm in names[2:]))
```
